```python
import jax, jax.numpy as jnp
from jax import lax
import numpy as np

D_MODEL = 1024
BATCH = 4
SEQ = 4096
DEPTH = 4
DEC_BATCH = 32
DEC_SEQ = 8
PAST_LEN = 8192
PAGE_SIZE = 128

N_EVEN = (DEPTH + 1) // 2
N_ODD = DEPTH // 2
H_A = 4
DK_A = 64
DV_A = 128
GATE_RANK = 16
GATE_NORM = 16.0
GLA_CHUNK = 64
DIL_PAIRS = ((128, 1), (512, 4), (2048, 16))
N_DIL = 3
H_BG = 4
DH_B = 64
H_C = 16
DH_C = 64
SB_BIAS_INIT = -6.0
D_FF = ((8 * D_MODEL // 3 + 255) // 256) * 256
Q_BLOCK = 128
EPS = 1e-6
EVEN_SPLIT = (H_A * DK_A, H_A * DK_A, H_A * DV_A, H_A * DV_A, GATE_RANK,
              N_DIL * H_BG * DH_B, N_DIL * H_BG * DH_B, N_DIL * H_BG * DH_B)
EVEN_IN = 2 * H_A * DK_A + 2 * H_A * DV_A + GATE_RANK + 3 * N_DIL * H_BG * DH_B
EVEN_OUT = H_A * DV_A + H_BG * DH_B
ODD_OUT = H_C * DH_C
ODD_IN = 3 * ODD_OUT

kernel_name = 'hybrid_gla_dilated_stickbreaking_decode_step'


def rms_norm(x, g):
    xf = x.astype(jnp.float32)
    y = xf * lax.rsqrt(jnp.mean(xf * xf, axis=-1, keepdims=True) + EPS)
    return (y * g.astype(jnp.float32)).astype(x.dtype)


def swiglu(x, w_gate, w_up, w_down):
    return (jax.nn.silu(x @ w_gate) * (x @ w_up)) @ w_down


def split_cols(a, sizes):
    parts, start = [], 0
    for s in sizes:
        parts.append(a[..., start:start + s])
        start += s
    return parts


def win_len(window):
    return min(window, PAST_LEN)


def last_rows(a, n):
    t = a.shape[1]
    if t >= n:
        return a[:, t - n:]
    return jnp.pad(a, [(0, 0), (n - t, 0)] + [(0, 0)] * (a.ndim - 2))


def map_query_blocks(fn, q, q_pos):
    b, t = q.shape[:2]
    if t <= Q_BLOCK or t % Q_BLOCK:
        return fn(q, q_pos)
    n = t // Q_BLOCK
    qb = jnp.moveaxis(q.reshape((b, n, Q_BLOCK) + q.shape[2:]), 1, 0)
    out = lax.map(lambda a: fn(a[0], a[1]), (qb, q_pos.reshape(n, Q_BLOCK)))
    return jnp.moveaxis(out, 0, 1).reshape((b, t) + out.shape[3:])


def gather_pages(pool, page_table):
    g = pool[page_table]
    return g.reshape((g.shape[0], -1) + g.shape[3:])


def gla_chunked(q, k, v, log_a, s0):
    b, t, h, _ = q.shape
    c = GLA_CHUNK if t % GLA_CHUNK == 0 else t
    n = t // c

    def chunks(a):
        return a.astype(jnp.float32).reshape(b, n, c, h, a.shape[-1]).transpose(1, 0, 3, 2, 4)

    causal = jnp.tril(jnp.ones((c, c), dtype=bool))[:, :, None]

    def step(s, inp):
        qi, ki, vi, ai = inp
        cum = jnp.cumsum(ai, axis=2)
        decay = jnp.exp(jnp.where(causal, cum[:, :, :, None, :] - cum[:, :, None, :, :], -jnp.inf))
        scores = jnp.einsum('bhtd,bhsd,bhtsd->bhts', qi, ki, decay)
        o = (jnp.einsum('bhts,bhsv->bhtv', scores, vi)
             + jnp.einsum('bhtd,bhdv->bhtv', qi * jnp.exp(cum), s))
        last = cum[:, :, -1]
        s = (jnp.exp(last)[..., None] * s
             + jnp.einsum('bhsd,bhsv->bhdv', ki * jnp.exp(last[:, :, None] - cum), vi))
        return s, o

    s, o = lax.scan(step, s0.astype(jnp.float32), tuple(chunks(a) for a in (q, k, v, log_a)))
    return o.transpose(1, 0, 3, 2, 4).reshape(b, t, h, -1), s.astype(s0.dtype)


def dilated_group_attend(q, kv, q_idx, window, dil):
    offs = dil * jnp.arange(window // dil + 1)
    idx = q_idx[:, None] - offs[None, :]
    valid = idx >= 0
    kvg = kv[:, jnp.maximum(idx, 0)]
    s = jnp.einsum('bqhd,bqjhd->bqhj', q, kvg[:, :, :, 0]).astype(jnp.float32) * (DH_B ** -0.5)
    s = jnp.where(valid[None, :, None, :], s, -jnp.inf)
    m = jnp.max(s, axis=-1, keepdims=True)
    p = jnp.exp(s - m)
    den = jnp.sum(p, axis=-1)
    o = jnp.einsum('bqhj,bqjhd->bqhd', p, kvg[:, :, :, 1].astype(jnp.float32)) / den[..., None]
    return o, m[..., 0] + jnp.log(den)


def stick_breaking_attend(q, k, v, bias, q_pos):
    z = (jnp.einsum('bqhd,bkhd->bhqk', q, k).astype(jnp.float32) * (DH_C ** -0.5)
         + bias.astype(jnp.float32)[None, :, None, None])
    mask = jnp.arange(k.shape[1])[None, :] < q_pos[:, None]
    neg_log_1m_beta = jnp.where(mask, jax.nn.softplus(z), 0.0)
    between = lax.cumsum(neg_log_1m_beta, axis=3, reverse=True) - neg_log_1m_beta
    weights = jnp.where(mask, jnp.exp(jax.nn.log_sigmoid(z) - between), 0.0)
    return jnp.einsum('bhqk,bkhd->bqhd', weights, v.astype(jnp.float32))


def even_mixer(h, w_in, gla_w_gate, gla_b_gate, gla_norm, w_out, s0, win_bufs):
    b, t, _ = h.shape
    q_a, k_a, v_a, r_a, g_a, q_b, k_b, v_b = split_cols(h @ w_in, EVEN_SPLIT)
    q_a = q_a.reshape(b, t, H_A, DK_A) * (DK_A ** -0.5)
    k_a = k_a.reshape(b, t, H_A, DK_A)
    v_a = v_a.reshape(b, t, H_A, DV_A)
    log_a = (jax.nn.log_sigmoid((g_a @ gla_w_gate + gla_b_gate).astype(jnp.float32))
             / GATE_NORM).reshape(b, t, H_A, DK_A)
    o_a, s_new = gla_chunked(q_a, k_a, v_a, log_a, s0)
    o_a = rms_norm(o_a, gla_norm) * jax.nn.silu(r_a.astype(jnp.float32)).reshape(b, t, H_A, DV_A)
    o_a = o_a.reshape(b, t, H_A * DV_A).astype(h.dtype)
    q_b = q_b.reshape(b, t, N_DIL, H_BG, DH_B)
    kv_b = jnp.stack([k_b.reshape(b, t, N_DIL, H_BG, DH_B),
                      v_b.reshape(b, t, N_DIL, H_BG, DH_B)], axis=3)
    ctx, offs, bufs_new = [], [], []
    for g, (window, _) in enumerate(DIL_PAIRS):
        kv_g = kv_b[:, :, g]
        if win_bufs is None:
            ctx.append(kv_g)
            offs.append(0)
            bufs_new.append(last_rows(kv_g, win_len(window)))
        else:
            full = jnp.concatenate([win_bufs[g], kv_g], axis=1)
            ctx.append(full)
            offs.append(win_bufs[g].shape[1])
            bufs_new.append(full[:, full.shape[1] - win_len(window):])

    def block(qb, pb):
        outs, lses = [], []
        for g, (window, dil) in enumerate(DIL_PAIRS):
            o, lse = dilated_group_attend(qb[:, :, g], ctx[g], pb + offs[g], window, dil)
            outs.append(o)
            lses.append(lse)
        wts = jax.nn.softmax(jnp.stack(lses), axis=0)
        return jnp.einsum('gbqh,gbqhd->bqhd', wts, jnp.stack(outs))

    o_b = map_query_blocks(block, q_b, jnp.arange(t)).reshape(b, t, H_BG * DH_B).astype(h.dtype)
    return jnp.concatenate([o_a, o_b], axis=-1) @ w_out, s_new, bufs_new


def odd_mixer(h, w_in, w_out, bias, past_k, past_v):
    b, t, _ = h.shape
    q, k, v = [a.reshape(b, t, H_C, DH_C) for a in split_cols(h @ w_in, (ODD_OUT, ODD_OUT, ODD_OUT))]
    if past_k is None:
        keys, vals, q_pos = k, v, jnp.arange(t)
    else:
        keys = jnp.concatenate([past_k, k], axis=1)
        vals = jnp.concatenate([past_v, v], axis=1)
        q_pos = past_k.shape[1] + jnp.arange(t)
    o = map_query_blocks(lambda qb, pb: stick_breaking_attend(qb, keys, vals, bias, pb), q, q_pos)
    return o.reshape(b, t, ODD_OUT).astype(h.dtype) @ w_out, k, v


def run_trunk(x, state_gla, win_caches, cache_sb_k, cache_sb_v, page_table,
              norm_pre, norm_post, ffn_w_gate, ffn_w_up, ffn_w_down,
              even_w_in, gla_w_gate, gla_b_gate, gla_norm, even_w_out, odd_w_in, odd_w_out, sb_bias):
    prompt = state_gla is None
    b = x.shape[0]
    gla_states, win_states = [], [[] for _ in DIL_PAIRS]
    sb_k_rows, sb_v_rows = [], []
    for l in range(DEPTH):
        i = l // 2
        x = x + 0.5 * rms_norm(swiglu(rms_norm(x, norm_pre[l, 0]), ffn_w_gate[l, 0], ffn_w_up[l, 0],
                                      ffn_w_down[l, 0]), norm_post[l, 0])
        h = rms_norm(x, norm_pre[l, 1])
        if l % 2 == 0:
            s0 = jnp.zeros((b, H_A, DK_A, DV_A), x.dtype) if prompt else state_gla[i]
            bufs = None if prompt else tuple(c[i] for c in win_caches)
            m, s_new, bufs_new = even_mixer(h, even_w_in[i], gla_w_gate[i], gla_b_gate[i], gla_norm[i],
                                            even_w_out[i], s0, bufs)
            gla_states.append(s_new)
            for g in range(N_DIL):
                win_states[g].append(bufs_new[g])
        else:
            past_k = None if prompt else gather_pages(cache_sb_k[i], page_table)
            past_v = None if prompt else gather_pages(cache_sb_v[i], page_table)
            m, k_new, v_new = odd_mixer(h, odd_w_in[i], odd_w_out[i], sb_bias[i], past_k, past_v)
            sb_k_rows.append(k_new)
            sb_v_rows.append(v_new)
        x = x + rms_norm(m, norm_post[l, 1])
        x = x + 0.5 * rms_norm(swiglu(rms_norm(x, norm_pre[l, 2]), ffn_w_gate[l, 1], ffn_w_up[l, 1],
                                      ffn_w_down[l, 1]), norm_post[l, 2])
    wins = [jnp.stack(w) for w in win_states]
    return x, jnp.stack(gla_states), wins, jnp.stack(sb_k_rows), jnp.stack(sb_v_rows)


def setup_inputs(seed: int = 0) -> dict:
    key = jax.random.key(seed)
    ks = jax.random.split(key, 24)

    def nrm(k, shape, scale=1.0):
        return jax.random.normal(k, shape, jnp.float32) * scale

    n_pages = PAST_LEN // PAGE_SIZE
    n_used = DEC_BATCH * n_pages
    n_pool = n_used + n_used // 4
    page_table = jax.random.permutation(ks[0], n_pool)[:n_used].reshape(DEC_BATCH, n_pages).astype(jnp.int32)
    win = [nrm(ks[1 + g], (N_EVEN, DEC_BATCH, win_len(w), 2, H_BG, DH_B)) for g, (w, _) in enumerate(DIL_PAIRS)]
    return {
        'x_prompt': nrm(ks[4], (BATCH, SEQ, D_MODEL)),
        'x_sample': nrm(ks[5], (DEC_BATCH, DEC_SEQ, D_MODEL)),
        'state_gla': nrm(ks[6], (N_EVEN, DEC_BATCH, H_A, DK_A, DV_A)),
        'cache_win_g0': win[0],
        'cache_win_g1': win[1],
        'cache_win_g2': win[2],
        'cache_sb_k': nrm(ks[7], (N_ODD, n_pool, PAGE_SIZE, H_C, DH_C)),
        'cache_sb_v': nrm(ks[8], (N_ODD, n_pool, PAGE_SIZE, H_C, DH_C)),
        'page_table': page_table,
        'norm_pre': 1.0 + nrm(ks[9], (DEPTH, 3, D_MODEL), 0.02),
        'norm_post': 1.0 + nrm(ks[10], (DEPTH, 3, D_MODEL), 0.02),
        'ffn_w_gate': nrm(ks[11], (DEPTH, 2, D_MODEL, D_FF), D_MODEL ** -0.5),
        'ffn_w_up': nrm(ks[12], (DEPTH, 2, D_MODEL, D_FF), D_MODEL ** -0.5),
        'ffn_w_down': nrm(ks[13], (DEPTH, 2, D_FF, D_MODEL), D_FF ** -0.5),
        'even_w_in': nrm(ks[14], (N_EVEN, D_MODEL, EVEN_IN), D_MODEL ** -0.5),
        'gla_w_gate': nrm(ks[15], (N_EVEN, GATE_RANK, H_A * DK_A), GATE_RANK ** -0.5),
        'gla_b_gate': nrm(ks[16], (N_EVEN, H_A * DK_A), 0.1),
        'gla_norm': 1.0 + nrm(ks[17], (N_EVEN, DV_A), 0.02),
        'even_w_out': nrm(ks[18], (N_EVEN, EVEN_OUT, D_MODEL), EVEN_OUT ** -0.5),
        'odd_w_in': nrm(ks[19], (N_ODD, D_MODEL, ODD_IN), D_MODEL ** -0.5),
        'odd_w_out': nrm(ks[20], (N_ODD, ODD_OUT, D_MODEL), ODD_OUT ** -0.5),
        'sb_bias': SB_BIAS_INIT + nrm(ks[21], (N_ODD, H_C), 0.1),
    }


def reference(x_prompt, x_sample, state_gla, cache_win_g0, cache_win_g1, cache_win_g2,
              cache_sb_k, cache_sb_v, page_table,
              norm_pre, norm_post, ffn_w_gate, ffn_w_up, ffn_w_down,
              even_w_in, gla_w_gate, gla_b_gate, gla_norm, even_w_out, odd_w_in, odd_w_out, sb_bias):
    y_prompt, gla_p, win_p, sb_k_p, sb_v_p = run_trunk(
        x_prompt, None, None, None, None, None,
        norm_pre, norm_post, ffn_w_gate, ffn_w_up, ffn_w_down,
        even_w_in, gla_w_gate, gla_b_gate, gla_norm, even_w_out, odd_w_in, odd_w_out, sb_bias)
    y_sample, gla_s, win_s, sb_k_s, sb_v_s = run_trunk(
        x_sample, state_gla, (cache_win_g0, cache_win_g1, cache_win_g2), cache_sb_k, cache_sb_v, page_table,
        norm_pre, norm_post, ffn_w_gate, ffn_w_up, ffn_w_down,
        even_w_in, gla_w_gate, gla_b_gate, gla_norm, even_w_out, odd_w_in, odd_w_out, sb_bias)
    return (y_prompt, y_sample, gla_p, gla_s, win_p[0], win_p[1], win_p[2], win_s[0], win_s[1], win_s[2],
            sb_k_p, sb_v_p, sb_k_s, sb_v_s)
```

```python
import functools

import jax
import jax.numpy as jnp
from jax import lax
from jax.experimental import pallas as pl
from jax.experimental.pallas import tpu as pltpu

f32 = jnp.float32
bf16 = jnp.bfloat16

EPS = 1e-6
H_A, DK_A, DV_A = 4, 64, 128
GATE_RANK = 16
GATE_NORM = 16.0
GLA_CHUNK = 64
DIL_PAIRS = ((128, 1), (512, 4), (2048, 16))
N_DIL = 3
H_BG, DH_B = 4, 64
H_C, DH_C = 16, 64
PAGE_SIZE = 128

A_Q, A_K, A_V, A_R = H_A * DK_A, H_A * DK_A, H_A * DV_A, H_A * DV_A
B_W = H_BG * DH_B
EVEN_MAIN = A_Q + A_K + A_V + A_R + 3 * N_DIL * B_W
COL_QB = A_Q + A_K + A_V + A_R
COL_KB = COL_QB + N_DIL * B_W
COL_VB = COL_KB + N_DIL * B_W
ODD_W = H_C * DH_C

LANE = 128
SUBLANE = 8
NEG = -1e30
VMEM_LIMIT = 56 * 1024 * 1024


def _cp(sem, vmem=None):
    return pltpu.CompilerParams(dimension_semantics=sem, vmem_limit_bytes=vmem)


def _rms(x, g):
    return x * lax.rsqrt(jnp.mean(x * x, axis=-1, keepdims=True) + EPS) * g


def _dot(a, b):
    return jnp.dot(a, b, preferred_element_type=f32)


def _dot_nt(a, b):
    return lax.dot_general(a, b, (((1,), (1,)), ((), ())), preferred_element_type=f32)


def _dot_tn(a, b):
    return lax.dot_general(a, b, (((0,), (0,)), ((), ())), preferred_element_type=f32)


def _softplus(z):
    return jnp.maximum(z, 0.0) + jnp.log1p(jnp.exp(-jnp.abs(z)))


def _split_dot(x, m01):
    hi = x.astype(bf16)
    lo = (x - hi.astype(f32)).astype(bf16)
    return _dot(hi, m01) + _dot(lo, m01)


def _row_tile(n):
    for t in (256, 128, 64, 32, 16, 8):
        if n % t == 0:
            return t
    raise ValueError(f"row count {n} is not a multiple of {SUBLANE}")


def _ffn_body(x_ref, gpre_ref, wg_ref, wu_ref, wd_ref, gpost_ref, o_ref):
    x = x_ref[...]
    h = _rms(x, gpre_ref[...]).astype(bf16)
    gate = _dot(h, wg_ref[...])
    up = _dot(h, wu_ref[...])
    a = (gate * jax.nn.sigmoid(gate) * up).astype(bf16)
    y = _dot(a, wd_ref[...])
    o_ref[...] = x + 0.5 * _rms(y, gpost_ref[...])


def _ffn(x, gpre, wg, wu, wd, gpost, l, j):
    n, d = x.shape
    ff = wg.shape[-1]
    tm = _row_tile(n)
    once = pl.Buffered(1)
    return pl.pallas_call(
        _ffn_body,
        grid=(n // tm,),
        in_specs=[
            pl.BlockSpec((tm, d), lambda i: (i, 0)),
            pl.BlockSpec((None, None, 1, d), lambda i: (l, 2 * j, 0, 0)),
            pl.BlockSpec((None, None, d, ff), lambda i: (l, j, 0, 0), pipeline_mode=once),
            pl.BlockSpec((None, None, d, ff), lambda i: (l, j, 0, 0), pipeline_mode=once),
            pl.BlockSpec((None, None, ff, d), lambda i: (l, j, 0, 0), pipeline_mode=once),
            pl.BlockSpec((None, None, 1, d), lambda i: (l, 2 * j, 0, 0)),
        ],
        out_specs=pl.BlockSpec((tm, d), lambda i: (i, 0)),
        out_shape=jax.ShapeDtypeStruct((n, d), f32),
        compiler_params=_cp(("arbitrary",), VMEM_LIMIT),
        name="ffn",
    )(x, gpre, wg, wu, wd, gpost)


def _proj_even_body(x_ref, g_ref, wm_ref, wgate_ref, gw_ref, gb_ref, p_ref, la_ref):
    h = _rms(x_ref[...], g_ref[...]).astype(bf16)
    p_ref[...] = _dot(h, wm_ref[...])
    ga = _dot(h, wgate_ref[...])
    lg = _dot(ga.astype(bf16), gw_ref[...]) + gb_ref[...]
    la_ref[...] = (jnp.minimum(lg, 0.0) - jnp.log1p(jnp.exp(-jnp.abs(lg)))) / GATE_NORM


def _proj_even(x, norm_pre, wm, wgate, gw, gb, l, i):
    n, d = x.shape
    tm = _row_tile(n)
    once = pl.Buffered(1)
    return pl.pallas_call(
        _proj_even_body,
        grid=(n // tm,),
        in_specs=[
            pl.BlockSpec((tm, d), lambda r: (r, 0)),
            pl.BlockSpec((None, None, 1, d), lambda r: (l, 1, 0, 0)),
            pl.BlockSpec((None, d, EVEN_MAIN), lambda r: (i, 0, 0), pipeline_mode=once),
            pl.BlockSpec((None, d, LANE), lambda r: (i, 0, 0), pipeline_mode=once),
            pl.BlockSpec((None, LANE, A_K), lambda r: (i, 0, 0), pipeline_mode=once),
            pl.BlockSpec((None, 1, A_K), lambda r: (i, 0, 0)),
        ],
        out_specs=[
            pl.BlockSpec((tm, EVEN_MAIN), lambda r: (r, 0)),
            pl.BlockSpec((tm, A_K), lambda r: (r, 0)),
        ],
        out_shape=[
            jax.ShapeDtypeStruct((n, EVEN_MAIN), f32),
            jax.ShapeDtypeStruct((n, A_K), f32),
        ],
        compiler_params=_cp(("arbitrary",), VMEM_LIMIT),
        name="proj_even",
    )(x, norm_pre, wm, wgate, gw, gb)


def _proj_odd_body(x_ref, g_ref, w_ref, p_ref):
    h = _rms(x_ref[...], g_ref[...]).astype(bf16)
    p_ref[...] = _dot(h, w_ref[...])


def _proj_odd(x, norm_pre, w, l, i):
    n, d = x.shape
    tm = _row_tile(n)
    wout = w.shape[-1]
    return pl.pallas_call(
        _proj_odd_body,
        grid=(n // tm,),
        in_specs=[
            pl.BlockSpec((tm, d), lambda r: (r, 0)),
            pl.BlockSpec((None, None, 1, d), lambda r: (l, 1, 0, 0)),
            pl.BlockSpec((None, d, wout), lambda r: (i, 0, 0), pipeline_mode=pl.Buffered(1)),
        ],
        out_specs=pl.BlockSpec((tm, wout), lambda r: (r, 0)),
        out_shape=jax.ShapeDtypeStruct((n, wout), f32),
        compiler_params=_cp(("arbitrary",), VMEM_LIMIT),
        name="proj_odd",
    )(x, norm_pre, w)


def _post_even_body(x_ref, oa_ref, o0_ref, o1_ref, o2_ref, l0_ref, l1_ref, l2_ref, w_ref, g_ref, y_ref):
    l0, l1, l2 = l0_ref[...], l1_ref[...], l2_ref[...]
    mx = jnp.maximum(jnp.maximum(l0, l1), l2)
    e0, e1, e2 = jnp.exp(l0 - mx), jnp.exp(l1 - mx), jnp.exp(l2 - mx)
    den = e0 + e1 + e2
    ob = (e0 / den) * o0_ref[...] + (e1 / den) * o1_ref[...] + (e2 / den) * o2_ref[...]
    m = _dot(oa_ref[...].astype(bf16), w_ref[0:A_V, :]) + _dot(ob.astype(bf16), w_ref[A_V:A_V + B_W, :])
    y_ref[...] = x_ref[...] + _rms(m, g_ref[...])


def _post_even(x, oa, og, lg, w_out, norm_post, l, i):
    n, d = x.shape
    tm = _row_tile(n)
    row = lambda w: pl.BlockSpec((tm, w), lambda r: (r, 0))
    return pl.pallas_call(
        _post_even_body,
        grid=(n // tm,),
        in_specs=[row(d), row(A_V)] + [row(B_W)] * 6 + [
            pl.BlockSpec((None, A_V + B_W, d), lambda r: (i, 0, 0), pipeline_mode=pl.Buffered(1)),
            pl.BlockSpec((None, None, 1, d), lambda r: (l, 1, 0, 0)),
        ],
        out_specs=row(d),
        out_shape=jax.ShapeDtypeStruct((n, d), f32),
        compiler_params=_cp(("arbitrary",), VMEM_LIMIT),
        name="post_even",
    )(x, oa, og[0], og[1], og[2], lg[0], lg[1], lg[2], w_out, norm_post)


def _post_odd_body(x_ref, o_ref, w_ref, g_ref, y_ref):
    m = _dot(o_ref[...].astype(bf16), w_ref[...])
    y_ref[...] = x_ref[...] + _rms(m, g_ref[...])


def _post_odd(x, o, w_out, norm_post, l, i):
    n, d = x.shape
    tm = _row_tile(n)
    row = lambda w: pl.BlockSpec((tm, w), lambda r: (r, 0))
    return pl.pallas_call(
        _post_odd_body,
        grid=(n // tm,),
        in_specs=[row(d), row(ODD_W),
                  pl.BlockSpec((None, ODD_W, d), lambda r: (i, 0, 0), pipeline_mode=pl.Buffered(1)),
                  pl.BlockSpec((None, None, 1, d), lambda r: (l, 1, 0, 0))],
        out_specs=row(d),
        out_shape=jax.ShapeDtypeStruct((n, d), f32),
        compiler_params=_cp(("arbitrary",), VMEM_LIMIT),
        name="post_odd",
    )(x, o, w_out, norm_post)


def _gla_body(*refs, c, aliased):
    if aliased:
        q_ref, k_ref, v_ref, r_ref, la_ref, s0_ref, gn_ref, _, o_ref, s_ref = refs
    else:
        q_ref, k_ref, v_ref, r_ref, la_ref, s0_ref, gn_ref, o_ref, s_ref = refs
    n = pl.program_id(1)

    @pl.when(n == 0)
    def _():
        s_ref[...] = s0_ref[...]

    q = q_ref[...] * (DK_A ** -0.5)
    k = k_ref[...]
    v = v_ref[...]
    r = r_ref[...]
    row = lax.broadcasted_iota(jnp.int32, (c, c), 0)
    col = lax.broadcasted_iota(jnp.int32, (c, c), 1)
    causal = col <= row
    la = la_ref[...]
    la_hi = la.astype(bf16)
    la_lo = (la - la_hi.astype(f32)).astype(bf16)
    ltri = causal.astype(bf16)
    cum = _dot(ltri, la_hi) + _dot(ltri, la_lo)
    mid = cum[c // 2:c // 2 + 1, :]
    last = cum[c - 1:c, :]
    q_in = (q * jnp.exp(cum - mid)).astype(bf16)
    k_in = (k * jnp.exp(mid - cum)).astype(bf16)
    q_st = (q * jnp.exp(cum)).astype(bf16)
    k_st = (k * jnp.exp(last - cum)).astype(bf16)
    e_last = jnp.exp(last)
    eye = (lax.broadcasted_iota(jnp.int32, (DK_A, DK_A), 0) == lax.broadcasted_iota(jnp.int32, (DK_A, DK_A), 1))
    gn = gn_ref[...]
    for h in range(H_A):
        sk = slice(h * DK_A, (h + 1) * DK_A)
        sv = slice(h * DV_A, (h + 1) * DV_A)
        s_old = s_ref[h]
        vh = v[:, sv].astype(bf16)
        scores = jnp.where(causal, _dot_nt(q_in[:, sk], k_in[:, sk]), 0.0)
        o = _dot(scores.astype(bf16), vh) + _dot(q_st[:, sk], s_old.astype(bf16))
        e_col = jnp.sum(jnp.where(eye, jnp.broadcast_to(e_last[:, sk], (DK_A, DK_A)), 0.0), axis=1, keepdims=True)
        s_ref[h] = e_col * s_old + _dot_tn(k_st[:, sk], vh)
        rh = r[:, sv]
        o_ref[:, sv] = _rms(o, gn) * (rh * jax.nn.sigmoid(rh))


def _gla(p, la, s0, gn, i, nb, nchunk, c, row0, o_prev=None):
    n = p.shape[0]
    rb0 = row0 // c
    rowmap = lambda cb: (lambda b, t: (rb0 + b * nchunk + t, cb))
    in_specs = [
        pl.BlockSpec((c, A_Q), rowmap(0)),
        pl.BlockSpec((c, A_K), rowmap(1)),
        pl.BlockSpec((c, A_V), rowmap(1)),
        pl.BlockSpec((c, A_R), rowmap(2)),
        pl.BlockSpec((c, A_K), rowmap(0)),
        pl.BlockSpec((None, H_A, DK_A, DV_A), lambda b, t: (b, 0, 0, 0)),
        pl.BlockSpec((None, 1, DV_A), lambda b, t: (i, 0, 0)),
    ]
    args = [p, p, p, p, la, s0, gn]
    aliases = {}
    if o_prev is not None:
        in_specs.append(pl.BlockSpec(memory_space=pl.ANY))
        args.append(o_prev)
        aliases = {len(args) - 1: 0}
    return pl.pallas_call(
        functools.partial(_gla_body, c=c, aliased=o_prev is not None),
        grid=(nb, nchunk),
        in_specs=in_specs,
        out_specs=[
            pl.BlockSpec((c, A_V), rowmap(0)),
            pl.BlockSpec((None, H_A, DK_A, DV_A), lambda b, t: (b, 0, 0, 0)),
        ],
        out_shape=[
            jax.ShapeDtypeStruct((n, A_V), f32),
            jax.ShapeDtypeStruct((nb, H_A, DK_A, DV_A), f32),
        ],
        input_output_aliases=aliases,
        compiler_params=_cp(("arbitrary", "arbitrary")),
        name="gla",
    )(*args)


def _head_lane(width):
    return lax.broadcasted_iota(jnp.int32, (1, width), 1) // DH_B


def _dil_prompt_body(q_ref, k_ref, v_ref, o_ref, lse_ref, *, nblk, qb):
    row = lax.broadcasted_iota(jnp.int32, (qb, qb), 0)
    col = lax.broadcasted_iota(jnp.int32, (qb, qb), 1)
    lane = _head_lane(B_W)

    def blk(i, carry):
        q0 = pl.multiple_of(i * qb, qb)
        p0 = pl.multiple_of(jnp.maximum(i - 1, 0) * qb, qb)
        q = q_ref[pl.ds(q0, qb), :] * (DH_B ** -0.5)
        kd = k_ref[pl.ds(q0, qb), :].astype(bf16)
        vd = v_ref[pl.ds(q0, qb), :].astype(bf16)
        kp = k_ref[pl.ds(p0, qb), :].astype(bf16)
        vp = v_ref[pl.ds(p0, qb), :].astype(bf16)
        mask_d = col <= row
        mask_p = jnp.logical_and(col >= row, i > 0)
        o_acc = jnp.zeros((qb, B_W), f32)
        l_acc = jnp.zeros((qb, B_W), f32)
        for h in range(H_BG):
            sel = lane == h
            qh = jnp.where(sel, q, 0.0).astype(bf16)
            sd = jnp.where(mask_d, _dot_nt(qh, kd), NEG)
            sp = jnp.where(mask_p, _dot_nt(qh, kp), NEG)
            m = jnp.maximum(jnp.max(sd, axis=-1, keepdims=True), jnp.max(sp, axis=-1, keepdims=True))
            pd = jnp.exp(sd - m)
            pp = jnp.exp(sp - m)
            den = jnp.sum(pd, axis=-1, keepdims=True) + jnp.sum(pp, axis=-1, keepdims=True)
            num = _dot(pd.astype(bf16), vd) + _dot(pp.astype(bf16), vp)
            o_acc = jnp.where(sel, num / den, o_acc)
            l_acc = jnp.where(sel, m + jnp.log(den), l_acc)
        o_ref[pl.ds(q0, qb), :] = o_acc
        lse_ref[pl.ds(q0, qb), :] = l_acc
        return carry

    lax.fori_loop(0, nblk, blk, 0)


def _dil_prompt(p, g, nb, t):
    window, dil = DIL_PAIRS[g]
    assert window // dil == LANE and t % (dil * LANE) == 0
    n = p.shape[0]
    ts = t // dil
    qb = LANE
    ncol = EVEN_MAIN // B_W
    pv = p.reshape(n // dil, dil * EVEN_MAIN)
    spec = lambda cb: pl.BlockSpec((ts, B_W), lambda b, r: (b, r * ncol + cb))
    o, lse = pl.pallas_call(
        functools.partial(_dil_prompt_body, nblk=ts // qb, qb=qb),
        grid=(nb, dil),
        in_specs=[spec(COL_QB // B_W + g), spec(COL_KB // B_W + g), spec(COL_VB // B_W + g)],
        out_specs=[pl.BlockSpec((ts, B_W), lambda b, r: (b, r))] * 2,
        out_shape=[jax.ShapeDtypeStruct((n // dil, dil * B_W), f32)] * 2,
        compiler_params=_cp(("arbitrary", "arbitrary"), VMEM_LIMIT),
        name=f"dil_prompt_g{g}",
    )(pv, pv, pv)
    return o.reshape(n, B_W), lse.reshape(n, B_W)


def _dil_sample_body(*refs, clen, window, dil, dt):
    c_ref, q_ref, kn_ref, vn_ref = refs[:4]
    oc_ref, o_ref, lse_ref = refs[-3:]
    kn = kn_ref[...]
    vn = vn_ref[...]
    oc_ref[:, 0:clen - dt] = c_ref[:, dt:clen]
    oc_ref[:, clen - dt:clen] = jnp.concatenate([kn, vn], axis=1).T

    lane = _head_lane(B_W)
    q = q_ref[...] * (DH_B ** -0.5)
    qs = jnp.concatenate([jnp.where(lane == h, q, 0.0) for h in range(H_BG)], axis=0).astype(bf16)
    nq = H_BG * dt
    kc = c_ref[0:B_W, :].astype(bf16)
    vc = c_ref[B_W:2 * B_W, :].astype(bf16)
    qi_c = jnp.bitwise_and(lax.broadcasted_iota(jnp.int32, (nq, clen), 0), dt - 1)
    dist_c = clen + qi_c - lax.broadcasted_iota(jnp.int32, (nq, clen), 1)
    ok_c = jnp.logical_and(jnp.bitwise_and(dist_c, dil - 1) == 0, dist_c <= window)
    qi_n = jnp.bitwise_and(lax.broadcasted_iota(jnp.int32, (nq, dt), 0), dt - 1)
    dist_n = qi_n - lax.broadcasted_iota(jnp.int32, (nq, dt), 1)
    ok_n = jnp.logical_and(dist_n >= 0, jnp.bitwise_and(dist_n, dil - 1) == 0)
    sc = jnp.where(ok_c, _dot(qs, kc), NEG)
    sn = jnp.where(ok_n, _dot_nt(qs, kn.astype(bf16)), NEG)
    m = jnp.maximum(jnp.max(sc, axis=-1, keepdims=True), jnp.max(sn, axis=-1, keepdims=True))
    pc = jnp.exp(sc - m)
    pn = jnp.exp(sn - m)
    den = jnp.sum(pc, axis=-1, keepdims=True) + jnp.sum(pn, axis=-1, keepdims=True)
    num = _dot_nt(pc.astype(bf16), vc) + _dot(pn.astype(bf16), vn.astype(bf16))
    o = num / den
    lse = m + jnp.log(den)
    o_out = jnp.zeros((dt, B_W), f32)
    l_out = jnp.zeros((dt, B_W), f32)
    for h in range(H_BG):
        sel = lane == h
        o_out = jnp.where(sel, o[h * dt:(h + 1) * dt, :], o_out)
        l_out = jnp.where(sel, jnp.broadcast_to(lse[h * dt:(h + 1) * dt, :], (dt, B_W)), l_out)
    o_ref[...] = o_out
    lse_ref[...] = l_out


def _dil_sample(p, cache_t, cache_out, o_prev, l_prev, g, i, db, dt, row0):
    window, dil = DIL_PAIRS[g]
    n_even, _, feat, clen = cache_t.shape
    n = p.shape[0]
    rb0 = row0 // dt
    rows = lambda cb: pl.BlockSpec((dt, B_W), lambda b: (rb0 + b, cb))
    any_spec = pl.BlockSpec(memory_space=pl.ANY)
    args = [cache_t, p, p, p, o_prev, l_prev]
    in_specs = [
        pl.BlockSpec((None, None, feat, clen), lambda b: (i, b, 0, 0)),
        rows(COL_QB // B_W + g), rows(COL_KB // B_W + g), rows(COL_VB // B_W + g),
        any_spec, any_spec,
    ]
    aliases = {4: 1, 5: 2}
    if cache_out is not None:
        args.append(cache_out)
        in_specs.append(any_spec)
        aliases[6] = 0
    return pl.pallas_call(
        functools.partial(_dil_sample_body, clen=clen, window=window, dil=dil, dt=dt),
        grid=(db,),
        in_specs=in_specs,
        out_specs=[
            pl.BlockSpec((None, None, feat, clen), lambda b: (i, b, 0, 0)),
            pl.BlockSpec((dt, B_W), lambda b: (rb0 + b, 0)),
            pl.BlockSpec((dt, B_W), lambda b: (rb0 + b, 0)),
        ],
        out_shape=[
            jax.ShapeDtypeStruct((n_even, db, feat, clen), f32),
            jax.ShapeDtypeStruct((n, B_W), f32),
            jax.ShapeDtypeStruct((n, B_W), f32),
        ],
        input_output_aliases=aliases,
        compiler_params=_cp(("arbitrary",), VMEM_LIMIT),
        name=f"dil_sample_g{g}",
    )(*args)


def _sb_weights(z, c, valid, umat):
    sp = _softplus(z)
    if valid is not None:
        sp = jnp.where(valid, sp, 0.0)
    between = _split_dot(sp, umat) + c
    w = jnp.exp(z - sp - between)
    if valid is not None:
        w = jnp.where(valid, w, 0.0)
    return w.astype(bf16), jnp.sum(sp, axis=-1, keepdims=True)


def _later_key_matrix(tk):
    return (lax.broadcasted_iota(jnp.int32, (tk, tk), 0) > lax.broadcasted_iota(jnp.int32, (tk, tk), 1)).astype(bf16)


def _sb_prompt_body(bias_ref, q_ref, k_ref, v_ref, o_ref, *, tq):
    hp = pl.program_id(1)
    qi = pl.program_id(2)
    lane = lax.broadcasted_iota(jnp.int32, (1, 2 * DH_C), 1) // DH_C
    q = q_ref[...] * (DH_C ** -0.5)
    rowi = lax.broadcasted_iota(jnp.int32, (tq, tq), 0)
    coli = lax.broadcasted_iota(jnp.int32, (tq, tq), 1)
    umat = _later_key_matrix(tq)
    outs = []
    for hh in range(2):
        qh = jnp.where(lane == hh, q, 0.0).astype(bf16)
        bias = bias_ref[2 * hp + hh]

        def kstep(t, carry):
            c, acc = carry
            kb = qi - t
            k0 = pl.multiple_of(kb * tq, tq)
            kblk = k_ref[pl.ds(k0, tq), :].astype(bf16)
            vblk = v_ref[pl.ds(k0, tq), :].astype(bf16)
            valid = (k0 + coli) < (qi * tq + rowi)
            w, sps = _sb_weights(_dot_nt(qh, kblk) + bias, c, valid, umat)
            return c + sps, acc + _dot(w, vblk)

        _, acc = lax.fori_loop(0, qi + 1, kstep, (jnp.zeros((tq, 1), f32), jnp.zeros((tq, 2 * DH_C), f32)))
        outs.append(acc)
    o_ref[...] = jnp.where(lane == 0, outs[0], outs[1])


def _sb_prompt(po, bias, nb, t):
    n = po.shape[0]
    tq = 256 if t % 256 == 0 else LANE
    assert t % tq == 0
    nq = t // tq
    nhp = H_C // 2
    w = 2 * DH_C
    return pl.pallas_call(
        functools.partial(_sb_prompt_body, tq=tq),
        grid_spec=pltpu.PrefetchScalarGridSpec(
            num_scalar_prefetch=1,
            grid=(nb, nhp, nq),
            in_specs=[
                pl.BlockSpec((tq, w), lambda b, hp, qi, bias: (b * nq + qi, hp)),
                pl.BlockSpec((t, w), lambda b, hp, qi, bias: (b, nhp + hp)),
                pl.BlockSpec((t, w), lambda b, hp, qi, bias: (b, 2 * nhp + hp)),
            ],
            out_specs=pl.BlockSpec((tq, w), lambda b, hp, qi, bias: (b * nq + qi, hp)),
        ),
        out_shape=jax.ShapeDtypeStruct((n, ODD_W), f32),
        compiler_params=_cp(("arbitrary", "arbitrary", "arbitrary"), VMEM_LIMIT),
        name="sb_prompt",
    )(bias, po, po, po)


def _sb_sample_body(pt_ref, q_ref, kn_ref, vn_ref, kc_ref, vc_ref, bias_ref, o_in, o_ref, qbd_ref, acc_ref, c_ref,
                    *, dt, npage):
    del pt_ref, o_in
    p = pl.program_id(1)
    nrow = H_C * dt
    umat = _later_key_matrix(PAGE_SIZE)
    bias = bias_ref[...]

    @pl.when(p == 0)
    def _():
        lane = lax.broadcasted_iota(jnp.int32, (1, ODD_W), 1) // DH_C
        q = q_ref[...] * (DH_C ** -0.5)
        qbd = jnp.concatenate([jnp.where(lane == h, q, 0.0) for h in range(H_C)], axis=0).astype(bf16)
        qbd_ref[...] = qbd
        pad = jnp.zeros((PAGE_SIZE - dt, ODD_W), f32)
        kblk = jnp.concatenate([kn_ref[...], pad], axis=0).astype(bf16)
        vblk = jnp.concatenate([vn_ref[...], pad], axis=0).astype(bf16)
        qidx = jnp.bitwise_and(lax.broadcasted_iota(jnp.int32, (nrow, PAGE_SIZE), 0), dt - 1)
        valid = lax.broadcasted_iota(jnp.int32, (nrow, PAGE_SIZE), 1) < qidx
        w, sps = _sb_weights(_dot_nt(qbd, kblk) + bias, jnp.zeros((nrow, 1), f32), valid, umat)
        acc_ref[...] = _dot(w, vblk)
        c_ref[...] = sps

    @pl.when(p > 0)
    def _():
        kt = kc_ref[...].reshape(ODD_W, PAGE_SIZE).astype(bf16)
        vt = vc_ref[...].reshape(ODD_W, PAGE_SIZE).astype(bf16)
        w, sps = _sb_weights(_dot(qbd_ref[...], kt) + bias, c_ref[...], None, umat)
        acc_ref[...] += _dot_nt(w, vt)
        c_ref[...] += sps

    @pl.when(p == npage)
    def _():
        lane = lax.broadcasted_iota(jnp.int32, (1, ODD_W), 1) // DH_C
        out = jnp.zeros((dt, ODD_W), f32)
        for h in range(H_C):
            out = jnp.where(lane == h, acc_ref[h * dt:(h + 1) * dt, :], out)
        o_ref[...] = out


def _sb_sample(po, cache_k, cache_v, page_table, bias_col, o_prev, i, db, dt, row0):
    n = po.shape[0]
    npage = page_table.shape[1]
    npool = cache_k.shape[1]
    kc = jnp.transpose(cache_k, (0, 1, 3, 4, 2))
    vc = jnp.transpose(cache_v, (0, 1, 3, 4, 2))
    rb0 = row0 // dt
    nrow = H_C * dt

    def page(b, p, pt):
        return (i, pt[b, npage - jnp.maximum(p, 1)], 0, 0, 0)

    rows = lambda cb: pl.BlockSpec((dt, ODD_W), lambda b, p, pt: (rb0 + b, cb))
    return pl.pallas_call(
        functools.partial(_sb_sample_body, dt=dt, npage=npage),
        grid_spec=pltpu.PrefetchScalarGridSpec(
            num_scalar_prefetch=1,
            grid=(db, npage + 1),
            in_specs=[
                rows(0), rows(1), rows(2),
                pl.BlockSpec((None, None, H_C, DH_C, PAGE_SIZE), page),
                pl.BlockSpec((None, None, H_C, DH_C, PAGE_SIZE), page),
                pl.BlockSpec((nrow, 1), lambda b, p, pt: (0, 0)),
                pl.BlockSpec(memory_space=pl.ANY),
            ],
            out_specs=pl.BlockSpec((dt, ODD_W), lambda b, p, pt: (rb0 + b, 0)),
            scratch_shapes=[
                pltpu.VMEM((nrow, ODD_W), bf16),
                pltpu.VMEM((nrow, ODD_W), f32),
                pltpu.VMEM((nrow, 1), f32),
            ],
        ),
        out_shape=jax.ShapeDtypeStruct((n, ODD_W), f32),
        input_output_aliases={7: 0},
        compiler_params=_cp(("arbitrary", "arbitrary")),
        name="sb_sample",
    )(page_table, po, po, po, kc, vc, bias_col, o_prev)


def _last_rows(a, n):
    t = a.shape[1]
    if t >= n:
        return a[:, t - n:]
    return jnp.pad(a, [(0, 0), (n - t, 0)] + [(0, 0)] * (a.ndim - 2))


def _win_rows(c):
    n_even, db, _, clen = c.shape
    return jnp.transpose(c.reshape(n_even, db, 2, H_BG, DH_B, clen), (0, 1, 5, 2, 3, 4))


def kernel(x_prompt, x_sample, state_gla, cache_win_g0, cache_win_g1, cache_win_g2, cache_sb_k, cache_sb_v, page_table, norm_pre, norm_post, ffn_w_gate, ffn_w_up, ffn_w_down, even_w_in, gla_w_gate, gla_b_gate, gla_norm, even_w_out, odd_w_in, odd_w_out, sb_bias):
    nb, t, d = x_prompt.shape
    db, dt, _ = x_sample.shape
    depth = norm_pre.shape[0]
    n_p, n_s = nb * t, db * dt
    win_caches = tuple(jnp.transpose(c, (0, 1, 3, 4, 5, 2)).reshape(c.shape[0], db, 2 * B_W, c.shape[2])
                       for c in (cache_win_g0, cache_win_g1, cache_win_g2))
    assert dt == SUBLANE and n_p % (16 * SUBLANE) == 0 and n_s % 16 == 0

    wg = ffn_w_gate.astype(bf16)
    wu = ffn_w_up.astype(bf16)
    wd = ffn_w_down.astype(bf16)
    g0, g1 = COL_QB, COL_QB + GATE_RANK
    w_main = jnp.concatenate([even_w_in[:, :, :g0], even_w_in[:, :, g1:]], axis=2).astype(bf16)
    w_gate_in = jnp.pad(even_w_in[:, :, g0:g1], ((0, 0), (0, 0), (0, LANE - GATE_RANK))).astype(bf16)
    gla_gw = jnp.pad(gla_w_gate, ((0, 0), (0, LANE - GATE_RANK), (0, 0))).astype(bf16)
    gla_gb = gla_b_gate[:, None, :]
    gla_gn = gla_norm[:, None, :]
    w_out_e = even_w_out.astype(bf16)
    w_in_o = odd_w_in.astype(bf16)
    w_out_o = odd_w_out.astype(bf16)
    npre = norm_pre[:, :, None, :]
    npost = norm_post[:, :, None, :]
    bias_cols = jnp.repeat(sb_bias, dt, axis=1)[:, :, None]

    x = jnp.concatenate([x_prompt.reshape(n_p, d), x_sample.reshape(n_s, d)], axis=0)
    c_p = GLA_CHUNK if t % GLA_CHUNK == 0 else t
    c_s = GLA_CHUNK if dt % GLA_CHUNK == 0 else dt

    gla_p, gla_s = [], []
    win_p, win_s = [[] for _ in DIL_PAIRS], [None for _ in DIL_PAIRS]
    sbk_p, sbv_p, sbk_s, sbv_s = [], [], [], []
    for l in range(depth):
        i = l // 2
        x = _ffn(x, npre, wg, wu, wd, npost, l, 0)
        if l % 2 == 0:
            p, la = _proj_even(x, npre, w_main, w_gate_in, gla_gw, gla_gb, l, i)
            oa, sp_new = _gla(p, la, jnp.zeros((nb, H_A, DK_A, DV_A), f32), gla_gn, i, nb, t // c_p, c_p, 0)
            oa, ss_new = _gla(p, la, state_gla[i], gla_gn, i, db, dt // c_s, c_s, n_p, o_prev=oa)
            gla_p.append(sp_new)
            gla_s.append(ss_new)
            og, lg = [], []
            for g, (window, _) in enumerate(DIL_PAIRS):
                o_g, l_g = _dil_prompt(p, g, nb, t)
                win_s[g], o_g, l_g = _dil_sample(p, win_caches[g], win_s[g], o_g, l_g, g, i, db, dt, n_p)
                og.append(o_g)
                lg.append(l_g)
                kcol, vcol = COL_KB + g * B_W, COL_VB + g * B_W
                kv = jnp.stack([p[:n_p, kcol:kcol + B_W].reshape(nb, t, H_BG, DH_B),
                                p[:n_p, vcol:vcol + B_W].reshape(nb, t, H_BG, DH_B)], axis=2)
                win_p[g].append(_last_rows(kv, min(window, page_table.shape[1] * PAGE_SIZE)))
            x = _post_even(x, oa, og, lg, w_out_e, npost, l, i)
        else:
            po = _proj_odd(x, npre, w_in_o, l, i)
            o = _sb_prompt(po, sb_bias[i], nb, t)
            o = _sb_sample(po, cache_sb_k, cache_sb_v, page_table, bias_cols[i], o, i, db, dt, n_p)
            sbk_p.append(po[:n_p, ODD_W:2 * ODD_W].reshape(nb, t, H_C, DH_C))
            sbv_p.append(po[:n_p, 2 * ODD_W:].reshape(nb, t, H_C, DH_C))
            sbk_s.append(po[n_p:, ODD_W:2 * ODD_W].reshape(db, dt, H_C, DH_C))
            sbv_s.append(po[n_p:, 2 * ODD_W:].reshape(db, dt, H_C, DH_C))
            x = _post_odd(x, o, w_out_o, npost, l, i)
        x = _ffn(x, npre, wg, wu, wd, npost, l, 1)

    return (x[:n_p].reshape(nb, t, d), x[n_p:].reshape(db, dt, d),
            jnp.stack(gla_p), jnp.stack(gla_s),
            jnp.stack(win_p[0]), jnp.stack(win_p[1]), jnp.stack(win_p[2]),
            _win_rows(win_s[0]), _win_rows(win_s[1]), _win_rows(win_s[2]),
            jnp.stack(sbk_p), jnp.stack(sbv_p), jnp.stack(sbk_s), jnp.stack(sbv_s))
```

```python
import functools

import jax
import jax.numpy as jnp
from jax import lax
from jax.experimental import pallas as pl
from jax.experimental.pallas import tpu as pltpu

f32 = jnp.float32
bf16 = jnp.bfloat16

EPS = 1e-6
H_A, DK_A, DV_A = 4, 64, 128
GATE_RANK = 16
GATE_NORM = 16.0
GLA_CHUNK = 64
DIL_PAIRS = ((128, 1), (512, 4), (2048, 16))
N_DIL = 3
H_BG, DH_B = 4, 64
H_C, DH_C = 16, 64
PAGE_SIZE = 128

A_Q, A_K, A_V, A_R = H_A * DK_A, H_A * DK_A, H_A * DV_A, H_A * DV_A
B_W = H_BG * DH_B
EVEN_MAIN = A_Q + A_K + A_V + A_R + 3 * N_DIL * B_W
COL_QB = A_Q + A_K + A_V + A_R
COL_KB = COL_QB + N_DIL * B_W
COL_VB = COL_KB + N_DIL * B_W
ODD_W = H_C * DH_C

LANE = 128
SUBLANE = 8
NEG = -1e30
VMEM_LIMIT = 56 * 1024 * 1024


def _cp(sem, vmem=None):
    return pltpu.CompilerParams(dimension_semantics=sem, vmem_limit_bytes=vmem)


def _rms(x, g):
    return x * lax.rsqrt(jnp.mean(x * x, axis=-1, keepdims=True) + EPS) * g


def _dot(a, b):
    return jnp.dot(a, b, preferred_element_type=f32)


def _dot_nt(a, b):
    return lax.dot_general(a, b, (((1,), (1,)), ((), ())), preferred_element_type=f32)


def _dot_tn(a, b):
    return lax.dot_general(a, b, (((0,), (0,)), ((), ())), preferred_element_type=f32)


def _softplus(z):
    return jnp.maximum(z, 0.0) + jnp.log1p(jnp.exp(-jnp.abs(z)))


def _split_dot(x, m01):
    hi = x.astype(bf16)
    lo = (x - hi.astype(f32)).astype(bf16)
    return _dot(hi, m01) + _dot(lo, m01)


def _row_tile(n):
    for t in (256, 128, 64, 32, 16, 8):
        if n % t == 0:
            return t
    raise ValueError(f"row count {n} is not a multiple of {SUBLANE}")


def _ffn_body(*refs, n_ptiles, split_in, split_out):
    r = pl.program_id(0)
    n_in = 2 if split_in else 1
    gpre_ref, wg_ref, wu_ref, wd_ref, gpost_ref = refs[n_in:n_in + 5]
    outs = refs[n_in + 5:]
    x = jnp.where(r < n_ptiles, refs[0][...], refs[1][...]) if split_in else refs[0][...]
    h = _rms(x, gpre_ref[...]).astype(bf16)
    gate = _dot(h, wg_ref[...])
    up = _dot(h, wu_ref[...])
    a = (gate * jax.nn.sigmoid(gate) * up).astype(bf16)
    y = x + 0.5 * _rms(_dot(a, wd_ref[...]), gpost_ref[...])
    if split_out:
        @pl.when(r < n_ptiles)
        def _():
            outs[0][...] = y

        @pl.when(r >= n_ptiles)
        def _():
            outs[1][...] = y
    else:
        outs[0][...] = y


def _ffn(xs, gpre, wg, wu, wd, gpost, l, j, n_p, split_out=False):
    split_in = isinstance(xs, tuple)
    xs = xs if split_in else (xs,)
    n = sum(a.shape[0] for a in xs)
    d = xs[0].shape[1]
    ff = wg.shape[-1]
    tm = _row_tile(n)
    assert n_p % tm == 0
    n_ptiles = n_p // tm
    once = pl.Buffered(1)
    prompt_rows = pl.BlockSpec((tm, d), lambda r: (jnp.minimum(r, n_ptiles - 1), 0))
    sample_rows = pl.BlockSpec((tm, d), lambda r: (jnp.maximum(r - n_ptiles, 0), 0))
    all_rows = pl.BlockSpec((tm, d), lambda r: (r, 0))
    if split_out:
        out_specs = [prompt_rows, sample_rows]
        out_shape = [jax.ShapeDtypeStruct((n_p, d), f32), jax.ShapeDtypeStruct((n - n_p, d), f32)]
    else:
        out_specs, out_shape = all_rows, jax.ShapeDtypeStruct((n, d), f32)
    return pl.pallas_call(
        functools.partial(_ffn_body, n_ptiles=n_ptiles, split_in=split_in, split_out=split_out),
        grid=(n // tm,),
        in_specs=([prompt_rows, sample_rows] if split_in else [all_rows]) + [
            pl.BlockSpec((None, None, 1, d), lambda r: (l, 2 * j, 0, 0)),
            pl.BlockSpec((None, None, d, ff), lambda r: (l, j, 0, 0), pipeline_mode=once),
            pl.BlockSpec((None, None, d, ff), lambda r: (l, j, 0, 0), pipeline_mode=once),
            pl.BlockSpec((None, None, ff, d), lambda r: (l, j, 0, 0), pipeline_mode=once),
            pl.BlockSpec((None, None, 1, d), lambda r: (l, 2 * j, 0, 0)),
        ],
        out_specs=out_specs,
        out_shape=out_shape,
        compiler_params=_cp(("arbitrary",), VMEM_LIMIT),
        name="ffn",
    )(*xs, gpre, wg, wu, wd, gpost)


def _proj_even_body(x_ref, g_ref, wm_ref, wgate_ref, gw_ref, gb_ref, p_ref, la_ref, *grp_refs):
    h = _rms(x_ref[...], g_ref[...]).astype(bf16)
    p = _dot(h, wm_ref[...])
    p_ref[...] = p
    for g, ref in enumerate(grp_refs, start=1):
        ref[...] = jnp.concatenate([p[:, c + g * B_W:c + (g + 1) * B_W] for c in (COL_QB, COL_KB, COL_VB)], axis=1)
    ga = _dot(h, wgate_ref[...])
    lg = _dot(ga.astype(bf16), gw_ref[...]) + gb_ref[...]
    la_ref[...] = (jnp.minimum(lg, 0.0) - jnp.log1p(jnp.exp(-jnp.abs(lg)))) / GATE_NORM


def _proj_even(x, norm_pre, wm, wgate, gw, gb, l, i):
    n, d = x.shape
    tm = _row_tile(n)
    once = pl.Buffered(1)
    return pl.pallas_call(
        _proj_even_body,
        grid=(n // tm,),
        in_specs=[
            pl.BlockSpec((tm, d), lambda r: (r, 0)),
            pl.BlockSpec((None, None, 1, d), lambda r: (l, 1, 0, 0)),
            pl.BlockSpec((None, d, EVEN_MAIN), lambda r: (i, 0, 0), pipeline_mode=once),
            pl.BlockSpec((None, d, LANE), lambda r: (i, 0, 0), pipeline_mode=once),
            pl.BlockSpec((None, LANE, A_K), lambda r: (i, 0, 0), pipeline_mode=once),
            pl.BlockSpec((None, 1, A_K), lambda r: (i, 0, 0)),
        ],
        out_specs=[
            pl.BlockSpec((tm, EVEN_MAIN), lambda r: (r, 0)),
            pl.BlockSpec((tm, A_K), lambda r: (r, 0)),
        ] + [pl.BlockSpec((tm, 3 * B_W), lambda r: (r, 0))] * (N_DIL - 1),
        out_shape=[
            jax.ShapeDtypeStruct((n, EVEN_MAIN), f32),
            jax.ShapeDtypeStruct((n, A_K), f32),
        ] + [jax.ShapeDtypeStruct((n, 3 * B_W), f32)] * (N_DIL - 1),
        compiler_params=_cp(("arbitrary",), VMEM_LIMIT),
        name="proj_even",
    )(x, norm_pre, wm, wgate, gw, gb)


def _proj_odd_body(*refs, n_ptiles):
    x_ref, g_ref, w_ref = refs[:3]
    q_ref, kt_ref, vt_ref, kvs_ref = refs[-4:]
    r = pl.program_id(0)
    h = _rms(x_ref[...], g_ref[...]).astype(bf16)
    p = _dot(h, w_ref[...])
    q_ref[...] = p[:, :ODD_W]

    @pl.when(r < n_ptiles)
    def _():
        kt_ref[...] = p[:, ODD_W:2 * ODD_W].T
        vt_ref[...] = p[:, 2 * ODD_W:].T

    @pl.when(r >= n_ptiles)
    def _():
        kvs_ref[...] = p[:, ODD_W:]


def _proj_odd(x, norm_pre, w, l, i, nb, t, kt_prev, vt_prev):
    n, d = x.shape
    tm = _row_tile(n)
    assert t % tm == 0
    n_odd = w.shape[0]
    tpb = t // tm
    n_ptiles = nb * tpb
    n_s = n - nb * t

    def kt_map(r):
        rp = jnp.minimum(r, n_ptiles - 1)
        return (i, rp // tpb, 0, rp % tpb)

    args = [x, norm_pre, w]
    in_specs = [
        pl.BlockSpec((tm, d), lambda r: (r, 0)),
        pl.BlockSpec((None, None, 1, d), lambda r: (l, 1, 0, 0)),
        pl.BlockSpec((None, d, 3 * ODD_W), lambda r: (i, 0, 0), pipeline_mode=pl.Buffered(1)),
    ]
    aliases = {}
    if kt_prev is not None:
        args += [kt_prev, vt_prev]
        in_specs += [pl.BlockSpec(memory_space=pl.ANY)] * 2
        aliases = {3: 1, 4: 2}
    stack = jax.ShapeDtypeStruct((n_odd, nb, ODD_W, t), f32)
    return pl.pallas_call(
        functools.partial(_proj_odd_body, n_ptiles=n_ptiles),
        grid=(n // tm,),
        in_specs=in_specs,
        out_specs=[
            pl.BlockSpec((tm, ODD_W), lambda r: (r, 0)),
            pl.BlockSpec((None, None, ODD_W, tm), kt_map),
            pl.BlockSpec((None, None, ODD_W, tm), kt_map),
            pl.BlockSpec((tm, 2 * ODD_W), lambda r: (jnp.maximum(r - n_ptiles, 0), 0)),
        ],
        out_shape=[jax.ShapeDtypeStruct((n, ODD_W), f32), stack, stack,
                   jax.ShapeDtypeStruct((n_s, 2 * ODD_W), f32)],
        input_output_aliases=aliases,
        compiler_params=_cp(("arbitrary",), VMEM_LIMIT),
        name="proj_odd",
    )(*args)


def _post_even_body(x_ref, oa_ref, o0_ref, o1_ref, o2_ref, l0_ref, l1_ref, l2_ref, w_ref, g_ref, y_ref):
    l0, l1, l2 = l0_ref[...], l1_ref[...], l2_ref[...]
    mx = jnp.maximum(jnp.maximum(l0, l1), l2)
    e0, e1, e2 = jnp.exp(l0 - mx), jnp.exp(l1 - mx), jnp.exp(l2 - mx)
    den = e0 + e1 + e2
    ob = (e0 / den) * o0_ref[...] + (e1 / den) * o1_ref[...] + (e2 / den) * o2_ref[...]
    m = _dot(oa_ref[...].astype(bf16), w_ref[0:A_V, :]) + _dot(ob.astype(bf16), w_ref[A_V:A_V + B_W, :])
    y_ref[...] = x_ref[...] + _rms(m, g_ref[...])


def _post_even(x, oa, og, lg, w_out, norm_post, l, i):
    n, d = x.shape
    tm = _row_tile(n)
    row = lambda w: pl.BlockSpec((tm, w), lambda r: (r, 0))
    return pl.pallas_call(
        _post_even_body,
        grid=(n // tm,),
        in_specs=[row(d), row(A_V)] + [row(B_W)] * 6 + [
            pl.BlockSpec((None, A_V + B_W, d), lambda r: (i, 0, 0), pipeline_mode=pl.Buffered(1)),
            pl.BlockSpec((None, None, 1, d), lambda r: (l, 1, 0, 0)),
        ],
        out_specs=row(d),
        out_shape=jax.ShapeDtypeStruct((n, d), f32),
        compiler_params=_cp(("arbitrary",), VMEM_LIMIT),
        name="post_even",
    )(x, oa, og[0], og[1], og[2], lg[0], lg[1], lg[2], w_out, norm_post)


def _post_odd_body(x_ref, o_ref, w_ref, g_ref, y_ref):
    m = _dot(o_ref[...].astype(bf16), w_ref[...])
    y_ref[...] = x_ref[...] + _rms(m, g_ref[...])


def _post_odd(x, o, w_out, norm_post, l, i):
    n, d = x.shape
    tm = _row_tile(n)
    row = lambda w: pl.BlockSpec((tm, w), lambda r: (r, 0))
    return pl.pallas_call(
        _post_odd_body,
        grid=(n // tm,),
        in_specs=[row(d), row(ODD_W),
                  pl.BlockSpec((None, ODD_W, d), lambda r: (i, 0, 0), pipeline_mode=pl.Buffered(1)),
                  pl.BlockSpec((None, None, 1, d), lambda r: (l, 1, 0, 0))],
        out_specs=row(d),
        out_shape=jax.ShapeDtypeStruct((n, d), f32),
        compiler_params=_cp(("arbitrary",), VMEM_LIMIT),
        name="post_odd",
    )(x, o, w_out, norm_post)


def _gla_body(*refs, c, aliased):
    if aliased:
        q_ref, k_ref, v_ref, r_ref, la_ref, s0_ref, gn_ref, _, o_ref, s_ref = refs
    else:
        q_ref, k_ref, v_ref, r_ref, la_ref, s0_ref, gn_ref, o_ref, s_ref = refs
    n = pl.program_id(1)

    @pl.when(n == 0)
    def _():
        s_ref[...] = s0_ref[...]

    q = q_ref[...] * (DK_A ** -0.5)
    k = k_ref[...]
    v = v_ref[...]
    r = r_ref[...]
    row = lax.broadcasted_iota(jnp.int32, (c, c), 0)
    col = lax.broadcasted_iota(jnp.int32, (c, c), 1)
    causal = col <= row
    la = la_ref[...]
    la_hi = la.astype(bf16)
    la_lo = (la - la_hi.astype(f32)).astype(bf16)
    ltri = causal.astype(bf16)
    cum = _dot(ltri, la_hi) + _dot(ltri, la_lo)
    mid = cum[c // 2:c // 2 + 1, :]
    last = cum[c - 1:c, :]
    q_in = (q * jnp.exp(cum - mid)).astype(bf16)
    k_in = (k * jnp.exp(mid - cum)).astype(bf16)
    q_st = (q * jnp.exp(cum)).astype(bf16)
    k_st = (k * jnp.exp(last - cum)).astype(bf16)
    e_last = jnp.exp(last)
    eye = (lax.broadcasted_iota(jnp.int32, (DK_A, DK_A), 0) == lax.broadcasted_iota(jnp.int32, (DK_A, DK_A), 1))
    gn = gn_ref[...]
    for h in range(H_A):
        sk = slice(h * DK_A, (h + 1) * DK_A)
        sv = slice(h * DV_A, (h + 1) * DV_A)
        s_old = s_ref[h]
        vh = v[:, sv].astype(bf16)
        scores = jnp.where(causal, _dot_nt(q_in[:, sk], k_in[:, sk]), 0.0)
        o = _dot(scores.astype(bf16), vh) + _dot(q_st[:, sk], s_old.astype(bf16))
        e_col = jnp.sum(jnp.where(eye, jnp.broadcast_to(e_last[:, sk], (DK_A, DK_A)), 0.0), axis=1, keepdims=True)
        s_ref[h] = e_col * s_old + _dot_tn(k_st[:, sk], vh)
        rh = r[:, sv]
        o_ref[:, sv] = _rms(o, gn) * (rh * jax.nn.sigmoid(rh))


def _gla(p, la, s0, gn, i, nb, nchunk, c, row0, o_prev=None):
    n = p.shape[0]
    rb0 = row0 // c
    rowmap = lambda cb: (lambda b, t: (rb0 + b * nchunk + t, cb))
    in_specs = [
        pl.BlockSpec((c, A_Q), rowmap(0)),
        pl.BlockSpec((c, A_K), rowmap(1)),
        pl.BlockSpec((c, A_V), rowmap(1)),
        pl.BlockSpec((c, A_R), rowmap(2)),
        pl.BlockSpec((c, A_K), rowmap(0)),
        pl.BlockSpec((None, H_A, DK_A, DV_A), lambda b, t: (b, 0, 0, 0)),
        pl.BlockSpec((None, 1, DV_A), lambda b, t: (i, 0, 0)),
    ]
    args = [p, p, p, p, la, s0, gn]
    aliases = {}
    if o_prev is not None:
        in_specs.append(pl.BlockSpec(memory_space=pl.ANY))
        args.append(o_prev)
        aliases = {len(args) - 1: 0}
    return pl.pallas_call(
        functools.partial(_gla_body, c=c, aliased=o_prev is not None),
        grid=(nb, nchunk),
        in_specs=in_specs,
        out_specs=[
            pl.BlockSpec((c, A_V), rowmap(0)),
            pl.BlockSpec((None, H_A, DK_A, DV_A), lambda b, t: (b, 0, 0, 0)),
        ],
        out_shape=[
            jax.ShapeDtypeStruct((n, A_V), f32),
            jax.ShapeDtypeStruct((nb, H_A, DK_A, DV_A), f32),
        ],
        input_output_aliases=aliases,
        compiler_params=_cp(("arbitrary", "arbitrary")),
        name="gla",
    )(*args)


def _head_lane(width):
    return lax.broadcasted_iota(jnp.int32, (1, width), 1) // DH_B


def _dil_prompt_body(q_ref, k_ref, v_ref, o_ref, lse_ref, *, nblk, qb):
    row = lax.broadcasted_iota(jnp.int32, (qb, qb), 0)
    col = lax.broadcasted_iota(jnp.int32, (qb, qb), 1)
    lane = _head_lane(B_W)

    def blk(i, carry):
        q0 = pl.multiple_of(i * qb, qb)
        p0 = pl.multiple_of(jnp.maximum(i - 1, 0) * qb, qb)
        q = q_ref[pl.ds(q0, qb), :] * (DH_B ** -0.5)
        kd = k_ref[pl.ds(q0, qb), :].astype(bf16)
        vd = v_ref[pl.ds(q0, qb), :].astype(bf16)
        kp = k_ref[pl.ds(p0, qb), :].astype(bf16)
        vp = v_ref[pl.ds(p0, qb), :].astype(bf16)
        mask_d = col <= row
        mask_p = jnp.logical_and(col >= row, i > 0)
        o_acc = jnp.zeros((qb, B_W), f32)
        l_acc = jnp.zeros((qb, B_W), f32)
        for h in range(H_BG):
            sel = lane == h
            qh = jnp.where(sel, q, 0.0).astype(bf16)
            sd = jnp.where(mask_d, _dot_nt(qh, kd), NEG)
            sp = jnp.where(mask_p, _dot_nt(qh, kp), NEG)
            m = jnp.maximum(jnp.max(sd, axis=-1, keepdims=True), jnp.max(sp, axis=-1, keepdims=True))
            pd = jnp.exp(sd - m)
            pp = jnp.exp(sp - m)
            den = jnp.sum(pd, axis=-1, keepdims=True) + jnp.sum(pp, axis=-1, keepdims=True)
            num = _dot(pd.astype(bf16), vd) + _dot(pp.astype(bf16), vp)
            o_acc = jnp.where(sel, num / den, o_acc)
            l_acc = jnp.where(sel, m + jnp.log(den), l_acc)
        o_ref[pl.ds(q0, qb), :] = o_acc
        lse_ref[pl.ds(q0, qb), :] = l_acc
        return carry

    lax.fori_loop(0, nblk, blk, 0)


def _dil_prompt(p, cols, g, nb, t):
    window, dil = DIL_PAIRS[g]
    assert window // dil == LANE and t % (dil * LANE) == 0
    n, width = p.shape
    ts = t // dil
    qb = LANE
    ncol = width // B_W
    pv = p.reshape(n // dil, dil * width)
    spec = lambda c: pl.BlockSpec((ts, B_W), lambda b, r: (b, r * ncol + c // B_W))
    o, lse = pl.pallas_call(
        functools.partial(_dil_prompt_body, nblk=ts // qb, qb=qb),
        grid=(nb, dil),
        in_specs=[spec(cols[0]), spec(cols[1]), spec(cols[2])],
        out_specs=[pl.BlockSpec((ts, B_W), lambda b, r: (b, r))] * 2,
        out_shape=[jax.ShapeDtypeStruct((n // dil, dil * B_W), f32)] * 2,
        compiler_params=_cp(("arbitrary", "arbitrary"), VMEM_LIMIT),
        name=f"dil_prompt_g{g}",
    )(pv, pv, pv)
    return o.reshape(n, B_W), lse.reshape(n, B_W)


def _dil_sample_body(*refs, clen, window, dil, dt):
    c_ref, q_ref, kn_ref, vn_ref = refs[:4]
    oc_ref, o_ref, lse_ref = refs[-3:]
    kn = kn_ref[...]
    vn = vn_ref[...]
    oc_ref[:, 0:clen - dt] = c_ref[:, dt:clen]
    oc_ref[:, clen - dt:clen] = jnp.concatenate([kn, vn], axis=1).T

    lane = _head_lane(B_W)
    q = q_ref[...] * (DH_B ** -0.5)
    qs = jnp.concatenate([jnp.where(lane == h, q, 0.0) for h in range(H_BG)], axis=0).astype(bf16)
    nq = H_BG * dt
    kc = c_ref[0:B_W, :].astype(bf16)
    vc = c_ref[B_W:2 * B_W, :].astype(bf16)
    qi_c = jnp.bitwise_and(lax.broadcasted_iota(jnp.int32, (nq, clen), 0), dt - 1)
    dist_c = clen + qi_c - lax.broadcasted_iota(jnp.int32, (nq, clen), 1)
    ok_c = jnp.logical_and(jnp.bitwise_and(dist_c, dil - 1) == 0, dist_c <= window)
    qi_n = jnp.bitwise_and(lax.broadcasted_iota(jnp.int32, (nq, dt), 0), dt - 1)
    dist_n = qi_n - lax.broadcasted_iota(jnp.int32, (nq, dt), 1)
    ok_n = jnp.logical_and(dist_n >= 0, jnp.bitwise_and(dist_n, dil - 1) == 0)
    sc = jnp.where(ok_c, _dot(qs, kc), NEG)
    sn = jnp.where(ok_n, _dot_nt(qs, kn.astype(bf16)), NEG)
    m = jnp.maximum(jnp.max(sc, axis=-1, keepdims=True), jnp.max(sn, axis=-1, keepdims=True))
    pc = jnp.exp(sc - m)
    pn = jnp.exp(sn - m)
    den = jnp.sum(pc, axis=-1, keepdims=True) + jnp.sum(pn, axis=-1, keepdims=True)
    num = _dot_nt(pc.astype(bf16), vc) + _dot(pn.astype(bf16), vn.astype(bf16))
    o = num / den
    lse = m + jnp.log(den)
    o_out = jnp.zeros((dt, B_W), f32)
    l_out = jnp.zeros((dt, B_W), f32)
    for h in range(H_BG):
        sel = lane == h
        o_out = jnp.where(sel, o[h * dt:(h + 1) * dt, :], o_out)
        l_out = jnp.where(sel, jnp.broadcast_to(lse[h * dt:(h + 1) * dt, :], (dt, B_W)), l_out)
    o_ref[...] = o_out
    lse_ref[...] = l_out


def _dil_sample(p, cache_t, cache_out, o_prev, l_prev, g, i, db, dt, row0):
    window, dil = DIL_PAIRS[g]
    n_even, _, feat, clen = cache_t.shape
    n = p.shape[0]
    rb0 = row0 // dt
    rows = lambda cb: pl.BlockSpec((dt, B_W), lambda b: (rb0 + b, cb))
    any_spec = pl.BlockSpec(memory_space=pl.ANY)
    args = [cache_t, p, p, p, o_prev, l_prev]
    in_specs = [
        pl.BlockSpec((None, None, feat, clen), lambda b: (i, b, 0, 0)),
        rows(COL_QB // B_W + g), rows(COL_KB // B_W + g), rows(COL_VB // B_W + g),
        any_spec, any_spec,
    ]
    aliases = {4: 1, 5: 2}
    if cache_out is not None:
        args.append(cache_out)
        in_specs.append(any_spec)
        aliases[6] = 0
    return pl.pallas_call(
        functools.partial(_dil_sample_body, clen=clen, window=window, dil=dil, dt=dt),
        grid=(db,),
        in_specs=in_specs,
        out_specs=[
            pl.BlockSpec((None, None, feat, clen), lambda b: (i, b, 0, 0)),
            pl.BlockSpec((dt, B_W), lambda b: (rb0 + b, 0)),
            pl.BlockSpec((dt, B_W), lambda b: (rb0 + b, 0)),
        ],
        out_shape=[
            jax.ShapeDtypeStruct((n_even, db, feat, clen), f32),
            jax.ShapeDtypeStruct((n, B_W), f32),
            jax.ShapeDtypeStruct((n, B_W), f32),
        ],
        input_output_aliases=aliases,
        compiler_params=_cp(("arbitrary",), VMEM_LIMIT),
        name=f"dil_sample_g{g}",
    )(*args)


def _sb_local(z, valid, umat):
    zpos = jnp.maximum(z, 0.0)
    zneg = jnp.minimum(z, 0.0)
    lg = jnp.log(1.0 + jnp.exp(zneg - zpos))
    sp = zpos + lg
    if valid is not None:
        sp = jnp.where(valid, sp, 0.0)
    expo = zneg - lg - _dot(sp.astype(bf16), umat)
    return expo, jnp.sum(sp, axis=-1, keepdims=True)


def _sb_finish(expo, c, valid):
    w = jnp.exp(expo - c)
    if valid is not None:
        w = jnp.where(valid, w, 0.0)
    return w.astype(bf16)


def _sb_weights(z, c, valid, umat):
    expo, sps = _sb_local(z, valid, umat)
    return _sb_finish(expo, c, valid), sps


def _later_key_matrix(tk):
    return (lax.broadcasted_iota(jnp.int32, (tk, tk), 0) > lax.broadcasted_iota(jnp.int32, (tk, tk), 1)).astype(bf16)


def _sb_prompt_body(bias_ref, q_ref, kt_ref, vt_ref, o_ref, kb_ref, vb_ref, *, tq, nblk):
    hp = pl.program_id(1)
    qi = pl.program_id(2)
    feat = lax.broadcasted_iota(jnp.int32, (2 * DH_C, 1), 0) // DH_C

    @pl.when(qi == 0)
    def _():
        for j in range(nblk):
            kb_ref[j] = kt_ref[:, j * tq:(j + 1) * tq].astype(bf16)
            vj = vt_ref[:, j * tq:(j + 1) * tq]
            for hh in range(2):
                vb_ref[hh, j] = jnp.where(feat == hh, vj, 0.0).astype(bf16)

    lane = lax.broadcasted_iota(jnp.int32, (1, 2 * DH_C), 1) // DH_C
    q = q_ref[...] * (DH_C ** -0.5)
    qh = [jnp.where(lane == hh, q, 0.0).astype(bf16) for hh in range(2)]
    bias = [bias_ref[2 * hp + hh] for hh in range(2)]
    umat = _later_key_matrix(tq)
    ahead = lax.broadcasted_iota(jnp.int32, (tq, tq), 1) - lax.broadcasted_iota(jnp.int32, (tq, tq), 0)

    def pair(s, carry, masked):
        cs, acc = carry
        new_cs = []
        for hh in range(2):
            parts = []
            for kb in (2 * s + 1, 2 * s):
                valid = (ahead + (kb - qi) * tq) < 0 if masked else None
                expo, sps = _sb_local(_dot(qh[hh], kb_ref[kb]) + bias[hh], valid, umat)
                parts.append((kb, expo, sps, valid))
            c = cs[hh]
            for kb, expo, sps, valid in parts:
                acc = acc + _dot_nt(_sb_finish(expo, c, valid), vb_ref[hh, kb])
                c = c + sps
            new_cs.append(c)
        return tuple(new_cs), acc

    zero_c = jnp.zeros((tq, 1), f32)
    sd = lax.shift_right_logical(qi, 1)
    carry = pair(sd, ((zero_c, zero_c), jnp.zeros((tq, 2 * DH_C), f32)), True)
    _, acc = lax.fori_loop(0, sd, lambda s, c: pair(sd - 1 - s, c, False), carry)
    o_ref[...] = acc


def _sb_prompt(q_rows, kt, vt, bias, i, nb, t):
    n = q_rows.shape[0]
    tq = 256 if t % 256 == 0 else LANE
    assert t % (2 * tq) == 0
    nq = t // tq
    nhp = H_C // 2
    w = 2 * DH_C
    return pl.pallas_call(
        functools.partial(_sb_prompt_body, tq=tq, nblk=nq),
        grid_spec=pltpu.PrefetchScalarGridSpec(
            num_scalar_prefetch=1,
            grid=(nb, nhp, nq),
            in_specs=[
                pl.BlockSpec((tq, w), lambda b, hp, qi, bias: (b * nq + qi, hp)),
                pl.BlockSpec((None, None, w, t), lambda b, hp, qi, bias: (i, b, hp, 0)),
                pl.BlockSpec((None, None, w, t), lambda b, hp, qi, bias: (i, b, hp, 0)),
            ],
            out_specs=pl.BlockSpec((tq, w), lambda b, hp, qi, bias: (b * nq + qi, hp)),
            scratch_shapes=[
                pltpu.VMEM((nq, w, tq), bf16),
                pltpu.VMEM((2, nq, w, tq), bf16),
            ],
        ),
        out_shape=jax.ShapeDtypeStruct((n, ODD_W), f32),
        compiler_params=_cp(("arbitrary", "arbitrary", "arbitrary"), VMEM_LIMIT),
        name="sb_prompt",
    )(bias, q_rows, kt, vt)


def _sb_sample_body(*refs, dt, nstep, group):
    q_ref, kvn_ref = refs[1:3]
    kc_refs = refs[3:3 + group]
    vc_refs = refs[3 + group:3 + 2 * group]
    bias_ref = refs[3 + 2 * group]
    o_ref, qbd_ref, acc_ref, c_ref = refs[-4:]
    p = pl.program_id(1)
    nrow = H_C * dt
    umat = _later_key_matrix(PAGE_SIZE)
    bias = bias_ref[...]

    @pl.when(p == 0)
    def _():
        lane = lax.broadcasted_iota(jnp.int32, (1, ODD_W), 1) // DH_C
        q = q_ref[...] * (DH_C ** -0.5)
        qbd = jnp.concatenate([jnp.where(lane == h, q, 0.0) for h in range(H_C)], axis=0).astype(bf16)
        qbd_ref[...] = qbd
        pad = jnp.zeros((PAGE_SIZE - dt, ODD_W), f32)
        kblk = jnp.concatenate([kvn_ref[:, 0:ODD_W], pad], axis=0).astype(bf16)
        vblk = jnp.concatenate([kvn_ref[:, ODD_W:2 * ODD_W], pad], axis=0).astype(bf16)
        qidx = jnp.bitwise_and(lax.broadcasted_iota(jnp.int32, (nrow, PAGE_SIZE), 0), dt - 1)
        valid = lax.broadcasted_iota(jnp.int32, (nrow, PAGE_SIZE), 1) < qidx
        w, sps = _sb_weights(_dot_nt(qbd, kblk) + bias, jnp.zeros((nrow, 1), f32), valid, umat)
        acc_ref[...] = _dot(w, vblk)
        c_ref[...] = sps

    @pl.when(p > 0)
    def _():
        qbd = qbd_ref[...]
        c = c_ref[...]
        ws = []
        for j in range(group):
            kt = kc_refs[j][...].reshape(ODD_W, PAGE_SIZE).astype(bf16)
            w, sps = _sb_weights(_dot(qbd, kt) + bias, c, None, umat)
            ws.append(w)
            c = c + sps
        c_ref[...] = c
        vt = jnp.concatenate([vc_refs[j][...].reshape(ODD_W, PAGE_SIZE).astype(bf16) for j in range(group)], axis=1)
        acc_ref[...] += _dot_nt(jnp.concatenate(ws, axis=1), vt)

    @pl.when(p == nstep)
    def _():
        lane = lax.broadcasted_iota(jnp.int32, (1, ODD_W), 1) // DH_C
        out = jnp.zeros((dt, ODD_W), f32)
        for h in range(H_C):
            out = jnp.where(lane == h, acc_ref[h * dt:(h + 1) * dt, :], out)
        o_ref[...] = out


def _sb_sample(q_rows, kv_new, cache_k, cache_v, page_table, bias_col, o_prev, i, db, dt, row0):
    n = q_rows.shape[0]
    npage = page_table.shape[1]
    group = 8 if npage % 8 == 0 else 1
    nstep = npage // group
    kc = jnp.transpose(cache_k, (0, 1, 3, 4, 2))
    vc = jnp.transpose(cache_v, (0, 1, 3, 4, 2))
    rb0 = row0 // dt
    nrow = H_C * dt

    def page(j):
        return lambda b, p, pt: (i, pt[b, npage - 1 - (jnp.maximum(p, 1) - 1) * group - j], 0, 0, 0)

    page_specs = [pl.BlockSpec((None, None, H_C, DH_C, PAGE_SIZE), page(j)) for j in range(group)]
    return pl.pallas_call(
        functools.partial(_sb_sample_body, dt=dt, nstep=nstep, group=group),
        grid_spec=pltpu.PrefetchScalarGridSpec(
            num_scalar_prefetch=1,
            grid=(db, nstep + 1),
            in_specs=[
                pl.BlockSpec((dt, ODD_W), lambda b, p, pt: (rb0 + b, 0)),
                pl.BlockSpec((dt, 2 * ODD_W), lambda b, p, pt: (b, 0)),
            ] + page_specs + page_specs + [
                pl.BlockSpec((nrow, 1), lambda b, p, pt: (0, 0)),
                pl.BlockSpec(memory_space=pl.ANY),
            ],
            out_specs=pl.BlockSpec((dt, ODD_W), lambda b, p, pt: (rb0 + b, 0)),
            scratch_shapes=[
                pltpu.VMEM((nrow, ODD_W), bf16),
                pltpu.VMEM((nrow, ODD_W), f32),
                pltpu.VMEM((nrow, 1), f32),
            ],
        ),
        out_shape=jax.ShapeDtypeStruct((n, ODD_W), f32),
        input_output_aliases={3 + 2 * group + 1: 0},
        compiler_params=_cp(("arbitrary", "arbitrary"), VMEM_LIMIT),
        name="sb_sample",
    )(page_table, q_rows, kv_new, *([kc] * group), *([vc] * group), bias_col, o_prev)


def _last_rows(a, n):
    t = a.shape[1]
    if t >= n:
        return a[:, t - n:]
    return jnp.pad(a, [(0, 0), (n - t, 0)] + [(0, 0)] * (a.ndim - 2))


def _token_major(a):
    n_odd, nb, _, t = a.shape
    return jnp.transpose(a.reshape(n_odd, nb, H_C, DH_C, t), (0, 1, 4, 2, 3))


def _win_rows(c):
    n_even, db, _, clen = c.shape
    return jnp.transpose(c.reshape(n_even, db, 2, H_BG, DH_B, clen), (0, 1, 5, 2, 3, 4))


def kernel(x_prompt, x_sample, state_gla, cache_win_g0, cache_win_g1, cache_win_g2, cache_sb_k, cache_sb_v, page_table, norm_pre, norm_post, ffn_w_gate, ffn_w_up, ffn_w_down, even_w_in, gla_w_gate, gla_b_gate, gla_norm, even_w_out, odd_w_in, odd_w_out, sb_bias):
    nb, t, d = x_prompt.shape
    db, dt, _ = x_sample.shape
    depth = norm_pre.shape[0]
    n_p, n_s = nb * t, db * dt
    win_caches = tuple(jnp.transpose(c, (0, 1, 3, 4, 5, 2)).reshape(c.shape[0], db, 2 * B_W, c.shape[2])
                       for c in (cache_win_g0, cache_win_g1, cache_win_g2))
    assert dt == SUBLANE and n_p % (16 * SUBLANE) == 0 and n_s % 16 == 0

    wg = ffn_w_gate.astype(bf16)
    wu = ffn_w_up.astype(bf16)
    wd = ffn_w_down.astype(bf16)
    g0, g1 = COL_QB, COL_QB + GATE_RANK
    w_main = jnp.concatenate([even_w_in[:, :, :g0], even_w_in[:, :, g1:]], axis=2).astype(bf16)
    w_gate_in = jnp.pad(even_w_in[:, :, g0:g1], ((0, 0), (0, 0), (0, LANE - GATE_RANK))).astype(bf16)
    gla_gw = jnp.pad(gla_w_gate, ((0, 0), (0, LANE - GATE_RANK), (0, 0))).astype(bf16)
    gla_gb = gla_b_gate[:, None, :]
    gla_gn = gla_norm[:, None, :]
    w_out_e = even_w_out.astype(bf16)
    w_in_o = odd_w_in.astype(bf16)
    w_out_o = odd_w_out.astype(bf16)
    npre = norm_pre[:, :, None, :]
    npost = norm_post[:, :, None, :]
    bias_cols = jnp.repeat(sb_bias, dt, axis=1)[:, :, None]

    x = (x_prompt.reshape(n_p, d), x_sample.reshape(n_s, d))
    c_p = GLA_CHUNK if t % GLA_CHUNK == 0 else t
    c_s = GLA_CHUNK if dt % GLA_CHUNK == 0 else dt

    gla_p, gla_s = [], []
    win_p, win_s = [[] for _ in DIL_PAIRS], [None for _ in DIL_PAIRS]
    kt_all, vt_all, sbk_s, sbv_s = None, None, [], []
    for l in range(depth):
        i = l // 2
        x = _ffn(x, npre, wg, wu, wd, npost, l, 0, n_p)
        if l % 2 == 0:
            p, la, *p_grp = _proj_even(x, npre, w_main, w_gate_in, gla_gw, gla_gb, l, i)
            oa, sp_new = _gla(p, la, jnp.zeros((nb, H_A, DK_A, DV_A), f32), gla_gn, i, nb, t // c_p, c_p, 0)
            oa, ss_new = _gla(p, la, state_gla[i], gla_gn, i, db, dt // c_s, c_s, n_p, o_prev=oa)
            gla_p.append(sp_new)
            gla_s.append(ss_new)
            og, lg = [], []
            for g, (window, _) in enumerate(DIL_PAIRS):
                if g == 0:
                    o_g, l_g = _dil_prompt(p, (COL_QB, COL_KB, COL_VB), g, nb, t)
                else:
                    o_g, l_g = _dil_prompt(p_grp[g - 1], (0, B_W, 2 * B_W), g, nb, t)
                win_s[g], o_g, l_g = _dil_sample(p, win_caches[g], win_s[g], o_g, l_g, g, i, db, dt, n_p)
                og.append(o_g)
                lg.append(l_g)
                kcol, vcol = COL_KB + g * B_W, COL_VB + g * B_W
                kv = jnp.stack([p[:n_p, kcol:kcol + B_W].reshape(nb, t, H_BG, DH_B),
                                p[:n_p, vcol:vcol + B_W].reshape(nb, t, H_BG, DH_B)], axis=2)
                win_p[g].append(_last_rows(kv, min(window, page_table.shape[1] * PAGE_SIZE)))
            x = _post_even(x, oa, og, lg, w_out_e, npost, l, i)
        else:
            q_rows, kt_all, vt_all, kv_new = _proj_odd(x, npre, w_in_o, l, i, nb, t, kt_all, vt_all)
            o = _sb_prompt(q_rows, kt_all, vt_all, sb_bias[i], i, nb, t)
            o = _sb_sample(q_rows, kv_new, cache_sb_k, cache_sb_v, page_table, bias_cols[i], o, i, db, dt, n_p)
            sbk_s.append(kv_new[:, :ODD_W].reshape(db, dt, H_C, DH_C))
            sbv_s.append(kv_new[:, ODD_W:].reshape(db, dt, H_C, DH_C))
            x = _post_odd(x, o, w_out_o, npost, l, i)
        x = _ffn(x, npre, wg, wu, wd, npost, l, 1, n_p, split_out=(l == depth - 1))

    return (x[0].reshape(nb, t, d), x[1].reshape(db, dt, d),
            jnp.stack(gla_p), jnp.stack(gla_s),
            jnp.stack(win_p[0]), jnp.stack(win_p[1]), jnp.stack(win_p[2]),
            _win_rows(win_s[0]), _win_rows(win_s[1]), _win_rows(win_s[2]),
            _token_major(kt_all), _token_major(vt_all), jnp.stack(sbk_s), jnp.stack(sbv_s))
```

```python
import functools
import math

import jax
import jax.numpy as jnp
from jax import lax
from jax.experimental import pallas as pl
from jax.experimental.pallas import tpu as pltpu

f32 = jnp.float32
bf16 = jnp.bfloat16

EPS = 1e-6
H_A, DK_A, DV_A = 4, 64, 128
GATE_RANK = 16
GATE_NORM = 16.0
GLA_CHUNK = 64
DIL_PAIRS = ((128, 1), (512, 4), (2048, 16))
N_DIL = 3
H_BG, DH_B = 4, 64
H_C, DH_C = 16, 64
PAGE_SIZE = 128

A_Q, A_K, A_V, A_R = H_A * DK_A, H_A * DK_A, H_A * DV_A, H_A * DV_A
B_W = H_BG * DH_B
EVEN_MAIN = A_Q + A_K + A_V + A_R + 3 * N_DIL * B_W
COL_QB = A_Q + A_K + A_V + A_R
COL_KB = COL_QB + N_DIL * B_W
COL_VB = COL_KB + N_DIL * B_W
ODD_W = H_C * DH_C

LANE = 128
SUBLANE = 8
NEG = -1e30
LOG2E = 1.4426950408889634
VMEM_LIMIT = 56 * 1024 * 1024
GLA_SEQS_PER_STEP = 4
SB_TQ, SB_TK = 512, 256


def _cp(sem, vmem=None):
    return pltpu.CompilerParams(dimension_semantics=sem, vmem_limit_bytes=vmem)


def _rms(x, g):
    return x * lax.rsqrt(jnp.mean(x * x, axis=-1, keepdims=True) + EPS) * g


def _dot(a, b):
    return jnp.dot(a, b, preferred_element_type=f32)


def _dot_nt(a, b):
    return lax.dot_general(a, b, (((1,), (1,)), ((), ())), preferred_element_type=f32)


def _dot_tn(a, b):
    return lax.dot_general(a, b, (((0,), (0,)), ((), ())), preferred_element_type=f32)


def _row_tile(n):
    for t in (256, 128, 64, 32, 16, 8):
        if n % t == 0:
            return t
    raise ValueError(f"row count {n} is not a multiple of {SUBLANE}")


def _ffn_body(*refs, n_ptiles, split_in, split_out):
    r = pl.program_id(0)
    n_in = 2 if split_in else 1
    gpre_ref, wg_ref, wu_ref, wd_ref, gpost_ref = refs[n_in:n_in + 5]
    outs = refs[n_in + 5:]
    x = jnp.where(r < n_ptiles, refs[0][...], refs[1][...]) if split_in else refs[0][...]
    h = _rms(x, gpre_ref[...]).astype(bf16)
    gate = _dot(h, wg_ref[...])
    up = _dot(h, wu_ref[...])
    a = (gate * jax.nn.sigmoid(gate) * up).astype(bf16)
    y = x + 0.5 * _rms(_dot(a, wd_ref[...]), gpost_ref[...])
    if split_out:
        @pl.when(r < n_ptiles)
        def _():
            outs[0][...] = y

        @pl.when(r >= n_ptiles)
        def _():
            outs[1][...] = y
    else:
        outs[0][...] = y


def _ffn(xs, gpre, wg, wu, wd, gpost, l, j, n_p, split_out=False):
    split_in = isinstance(xs, tuple)
    xs = xs if split_in else (xs,)
    n = sum(a.shape[0] for a in xs)
    d = xs[0].shape[1]
    ff = wg.shape[-1]
    tm = _row_tile(n)
    assert n_p % tm == 0
    n_ptiles = n_p // tm
    once = pl.Buffered(1)
    prompt_rows = pl.BlockSpec((tm, d), lambda r: (jnp.minimum(r, n_ptiles - 1), 0))
    sample_rows = pl.BlockSpec((tm, d), lambda r: (jnp.maximum(r - n_ptiles, 0), 0))
    all_rows = pl.BlockSpec((tm, d), lambda r: (r, 0))
    if split_out:
        out_specs = [prompt_rows, sample_rows]
        out_shape = [jax.ShapeDtypeStruct((n_p, d), f32), jax.ShapeDtypeStruct((n - n_p, d), f32)]
    else:
        out_specs, out_shape = all_rows, jax.ShapeDtypeStruct((n, d), f32)
    return pl.pallas_call(
        functools.partial(_ffn_body, n_ptiles=n_ptiles, split_in=split_in, split_out=split_out),
        grid=(n // tm,),
        in_specs=([prompt_rows, sample_rows] if split_in else [all_rows]) + [
            pl.BlockSpec((None, None, 1, d), lambda r: (l, 2 * j, 0, 0)),
            pl.BlockSpec((None, None, d, ff), lambda r: (l, j, 0, 0), pipeline_mode=once),
            pl.BlockSpec((None, None, d, ff), lambda r: (l, j, 0, 0), pipeline_mode=once),
            pl.BlockSpec((None, None, ff, d), lambda r: (l, j, 0, 0), pipeline_mode=once),
            pl.BlockSpec((None, None, 1, d), lambda r: (l, 2 * j, 0, 0)),
        ],
        out_specs=out_specs,
        out_shape=out_shape,
        compiler_params=_cp(("arbitrary",), VMEM_LIMIT),
        name="ffn",
    )(*xs, gpre, wg, wu, wd, gpost)


def _proj_even_body(x_ref, g_ref, wm_ref, wgate_ref, gw_ref, gb_ref, p_ref, la_ref, *grp_refs):
    h = _rms(x_ref[...], g_ref[...]).astype(bf16)
    p = _dot(h, wm_ref[...])
    p_ref[...] = p
    for g, ref in enumerate(grp_refs, start=1):
        ref[...] = jnp.concatenate([p[:, c + g * B_W:c + (g + 1) * B_W] for c in (COL_QB, COL_KB, COL_VB)], axis=1)
    ga = _dot(h, wgate_ref[...])
    lg = _dot(ga.astype(bf16), gw_ref[...]) + gb_ref[...]
    la_ref[...] = (jnp.minimum(lg, 0.0) - jnp.log1p(jnp.exp(-jnp.abs(lg)))) / GATE_NORM


def _proj_even(x, norm_pre, wm, wgate, gw, gb, l, i):
    n, d = x.shape
    tm = _row_tile(n)
    once = pl.Buffered(1)
    return pl.pallas_call(
        _proj_even_body,
        grid=(n // tm,),
        in_specs=[
            pl.BlockSpec((tm, d), lambda r: (r, 0)),
            pl.BlockSpec((None, None, 1, d), lambda r: (l, 1, 0, 0)),
            pl.BlockSpec((None, d, EVEN_MAIN), lambda r: (i, 0, 0), pipeline_mode=once),
            pl.BlockSpec((None, d, LANE), lambda r: (i, 0, 0), pipeline_mode=once),
            pl.BlockSpec((None, LANE, A_K), lambda r: (i, 0, 0), pipeline_mode=once),
            pl.BlockSpec((None, 1, A_K), lambda r: (i, 0, 0)),
        ],
        out_specs=[
            pl.BlockSpec((tm, EVEN_MAIN), lambda r: (r, 0)),
            pl.BlockSpec((tm, A_K), lambda r: (r, 0)),
        ] + [pl.BlockSpec((tm, 3 * B_W), lambda r: (r, 0))] * (N_DIL - 1),
        out_shape=[
            jax.ShapeDtypeStruct((n, EVEN_MAIN), f32),
            jax.ShapeDtypeStruct((n, A_K), f32),
        ] + [jax.ShapeDtypeStruct((n, 3 * B_W), f32)] * (N_DIL - 1),
        compiler_params=_cp(("arbitrary",), VMEM_LIMIT),
        name="proj_even",
    )(x, norm_pre, wm, wgate, gw, gb)


def _proj_odd_body(*refs, n_ptiles):
    x_ref, g_ref, w_ref = refs[:3]
    q_ref, kt_ref, vt_ref, kvs_ref = refs[-4:]
    r = pl.program_id(0)
    h = _rms(x_ref[...], g_ref[...]).astype(bf16)
    p = _dot(h, w_ref[...])
    q_ref[...] = p[:, :ODD_W]

    @pl.when(r < n_ptiles)
    def _():
        kt_ref[...] = p[:, ODD_W:2 * ODD_W].T
        vt_ref[...] = p[:, 2 * ODD_W:].T

    @pl.when(r >= n_ptiles)
    def _():
        kvs_ref[...] = p[:, ODD_W:]


def _proj_odd(x, norm_pre, w, l, i, nb, t, kt_prev, vt_prev):
    n, d = x.shape
    tm = _row_tile(n)
    assert t % tm == 0
    n_odd = w.shape[0]
    tpb = t // tm
    n_ptiles = nb * tpb
    n_s = n - nb * t

    def kt_map(r):
        rp = jnp.minimum(r, n_ptiles - 1)
        return (i, rp // tpb, 0, rp % tpb)

    args = [x, norm_pre, w]
    in_specs = [
        pl.BlockSpec((tm, d), lambda r: (r, 0)),
        pl.BlockSpec((None, None, 1, d), lambda r: (l, 1, 0, 0)),
        pl.BlockSpec((None, d, 3 * ODD_W), lambda r: (i, 0, 0), pipeline_mode=pl.Buffered(1)),
    ]
    aliases = {}
    if kt_prev is not None:
        args += [kt_prev, vt_prev]
        in_specs += [pl.BlockSpec(memory_space=pl.ANY)] * 2
        aliases = {3: 1, 4: 2}
    stack = jax.ShapeDtypeStruct((n_odd, nb, ODD_W, t), f32)
    return pl.pallas_call(
        functools.partial(_proj_odd_body, n_ptiles=n_ptiles),
        grid=(n // tm,),
        in_specs=in_specs,
        out_specs=[
            pl.BlockSpec((tm, ODD_W), lambda r: (r, 0)),
            pl.BlockSpec((None, None, ODD_W, tm), kt_map),
            pl.BlockSpec((None, None, ODD_W, tm), kt_map),
            pl.BlockSpec((tm, 2 * ODD_W), lambda r: (jnp.maximum(r - n_ptiles, 0), 0)),
        ],
        out_shape=[jax.ShapeDtypeStruct((n, ODD_W), f32), stack, stack,
                   jax.ShapeDtypeStruct((n_s, 2 * ODD_W), f32)],
        input_output_aliases=aliases,
        compiler_params=_cp(("arbitrary",), VMEM_LIMIT),
        name="proj_odd",
    )(*args)


def _post_even_body(x_ref, oap_ref, oas_ref, o0_ref, o1_ref, o2_ref, l0_ref, l1_ref, l2_ref, w_ref, g_ref, y_ref,
                    *, n_ptiles):
    oa = jnp.where(pl.program_id(0) < n_ptiles, oap_ref[...], oas_ref[...])
    l0, l1, l2 = l0_ref[...], l1_ref[...], l2_ref[...]
    mx = jnp.maximum(jnp.maximum(l0, l1), l2)
    e0, e1, e2 = jnp.exp(l0 - mx), jnp.exp(l1 - mx), jnp.exp(l2 - mx)
    den = e0 + e1 + e2
    ob = (e0 / den) * o0_ref[...] + (e1 / den) * o1_ref[...] + (e2 / den) * o2_ref[...]
    m = _dot(oa.astype(bf16), w_ref[0:A_V, :]) + _dot(ob.astype(bf16), w_ref[A_V:A_V + B_W, :])
    y_ref[...] = x_ref[...] + _rms(m, g_ref[...])


def _post_even(x, oa, og, lg, w_out, norm_post, l, i):
    n, d = x.shape
    tm = _row_tile(n)
    n_p = oa[0].shape[0]
    assert n_p % tm == 0 and oa[1].shape[0] % tm == 0
    n_ptiles = n_p // tm
    row = lambda w: pl.BlockSpec((tm, w), lambda r: (r, 0))
    oa_specs = [pl.BlockSpec((tm, A_V), lambda r: (jnp.minimum(r, n_ptiles - 1), 0)),
                pl.BlockSpec((tm, A_V), lambda r: (jnp.maximum(r - n_ptiles, 0), 0))]
    return pl.pallas_call(
        functools.partial(_post_even_body, n_ptiles=n_ptiles),
        grid=(n // tm,),
        in_specs=[row(d)] + oa_specs + [row(B_W)] * 6 + [
            pl.BlockSpec((None, A_V + B_W, d), lambda r: (i, 0, 0), pipeline_mode=pl.Buffered(1)),
            pl.BlockSpec((None, None, 1, d), lambda r: (l, 1, 0, 0)),
        ],
        out_specs=row(d),
        out_shape=jax.ShapeDtypeStruct((n, d), f32),
        compiler_params=_cp(("arbitrary",), VMEM_LIMIT),
        name="post_even",
    )(x, oa[0], oa[1], og[0], og[1], og[2], lg[0], lg[1], lg[2], w_out, norm_post)


def _post_odd_body(x_ref, o_ref, w_ref, g_ref, y_ref):
    m = _dot(o_ref[...].astype(bf16), w_ref[...])
    y_ref[...] = x_ref[...] + _rms(m, g_ref[...])


def _post_odd(x, o, w_out, norm_post, l, i):
    n, d = x.shape
    tm = _row_tile(n)
    row = lambda w: pl.BlockSpec((tm, w), lambda r: (r, 0))
    return pl.pallas_call(
        _post_odd_body,
        grid=(n // tm,),
        in_specs=[row(d), row(ODD_W),
                  pl.BlockSpec((None, ODD_W, d), lambda r: (i, 0, 0), pipeline_mode=pl.Buffered(1)),
                  pl.BlockSpec((None, None, 1, d), lambda r: (l, 1, 0, 0))],
        out_specs=row(d),
        out_shape=jax.ShapeDtypeStruct((n, d), f32),
        compiler_params=_cp(("arbitrary",), VMEM_LIMIT),
        name="post_odd",
    )(x, o, w_out, norm_post)


def _gla_body(*refs, c, bpg):
    s0_ref, gn_ref, o_ref, s_ref = refs[5 * bpg:]
    n = pl.program_id(1)

    @pl.when(n == 0)
    def _():
        s_ref[...] = s0_ref[...]

    row = lax.broadcasted_iota(jnp.int32, (c, c), 0)
    col = lax.broadcasted_iota(jnp.int32, (c, c), 1)
    causal = col <= row
    ltri = causal.astype(bf16)
    eye = (lax.broadcasted_iota(jnp.int32, (DK_A, DK_A), 0) == lax.broadcasted_iota(jnp.int32, (DK_A, DK_A), 1))
    gn = gn_ref[...]
    s_olds = [[s_ref[j, h] for h in range(H_A)] for j in range(bpg)]
    results = []
    for j in range(bpg):
        q_ref, k_ref, v_ref, r_ref, la_ref = refs[5 * j:5 * j + 5]
        q = q_ref[...] * (DK_A ** -0.5)
        k = k_ref[...]
        v = v_ref[...]
        r = r_ref[...]
        la = la_ref[...]
        la_hi = la.astype(bf16)
        la_lo = (la - la_hi.astype(f32)).astype(bf16)
        cum = _dot(ltri, la_hi) + _dot(ltri, la_lo)
        mid = cum[c // 2:c // 2 + 1, :]
        last = cum[c - 1:c, :]
        q_in = (q * jnp.exp(cum - mid)).astype(bf16)
        k_in = (k * jnp.exp(mid - cum)).astype(bf16)
        q_st = (q * jnp.exp(cum)).astype(bf16)
        k_st = (k * jnp.exp(last - cum)).astype(bf16)
        e_last = jnp.exp(last)
        for h in range(H_A):
            sk = slice(h * DK_A, (h + 1) * DK_A)
            sv = slice(h * DV_A, (h + 1) * DV_A)
            s_old = s_olds[j][h]
            vh = v[:, sv].astype(bf16)
            scores = jnp.where(causal, _dot_nt(q_in[:, sk], k_in[:, sk]), 0.0)
            o = _dot(scores.astype(bf16), vh) + _dot(q_st[:, sk], s_old.astype(bf16))
            e_col = jnp.sum(jnp.where(eye, jnp.broadcast_to(e_last[:, sk], (DK_A, DK_A)), 0.0), axis=1,
                            keepdims=True)
            s_new = e_col * s_old + _dot_tn(k_st[:, sk], vh)
            rh = r[:, sv]
            results.append((j, h, s_new, _rms(o, gn) * (rh * jax.nn.sigmoid(rh))))
    for j, h, s_new, o_gated in results:
        s_ref[j, h] = s_new
        o_ref[j, :, h * DV_A:(h + 1) * DV_A] = o_gated


def _gla(p, la, s0, gn, i, nb, nchunk, c, row0, bpg):
    assert nb % bpg == 0
    rb0 = row0 // c
    in_specs = []
    args = []
    for j in range(bpg):
        rowmap = lambda cb, j=j: (lambda g, t: (rb0 + (g * bpg + j) * nchunk + t, cb))
        in_specs += [
            pl.BlockSpec((c, A_Q), rowmap(0)),
            pl.BlockSpec((c, A_K), rowmap(1)),
            pl.BlockSpec((c, A_V), rowmap(1)),
            pl.BlockSpec((c, A_R), rowmap(2)),
            pl.BlockSpec((c, A_K), rowmap(0)),
        ]
        args += [p, p, p, p, la]
    in_specs += [
        pl.BlockSpec((bpg, H_A, DK_A, DV_A), lambda g, t: (g, 0, 0, 0)),
        pl.BlockSpec((None, 1, DV_A), lambda g, t: (i, 0, 0)),
    ]
    args += [s0, gn]
    o, s_new = pl.pallas_call(
        functools.partial(_gla_body, c=c, bpg=bpg),
        grid=(nb // bpg, nchunk),
        in_specs=in_specs,
        out_specs=[
            pl.BlockSpec((bpg, c, A_V), lambda g, t: (g, t, 0)),
            pl.BlockSpec((bpg, H_A, DK_A, DV_A), lambda g, t: (g, 0, 0, 0)),
        ],
        out_shape=[
            jax.ShapeDtypeStruct((nb, nchunk * c, A_V), f32),
            jax.ShapeDtypeStruct((nb, H_A, DK_A, DV_A), f32),
        ],
        compiler_params=_cp(("arbitrary", "arbitrary"), VMEM_LIMIT),
        name="gla",
    )(*args)
    return o.reshape(nb * nchunk * c, A_V), s_new


def _head_lane(width):
    return lax.broadcasted_iota(jnp.int32, (1, width), 1) // DH_B


def _dil_prompt_body(q_ref, k_ref, v_ref, o_ref, lse_ref, *, nblk, qb):
    row = jnp.bitwise_and(lax.broadcasted_iota(jnp.int32, (H_BG * qb, 2 * qb), 0), qb - 1)
    col = lax.broadcasted_iota(jnp.int32, (H_BG * qb, 2 * qb), 1)
    band = jnp.logical_and(col >= row, col <= row + qb)
    lane = _head_lane(B_W)

    def blk(i, carry):
        q0 = pl.multiple_of(i * qb, qb)
        p0 = pl.multiple_of(jnp.maximum(i - 1, 0) * qb, qb)
        q = q_ref[pl.ds(q0, qb), :] * (DH_B ** -0.5)
        qs = jnp.concatenate([jnp.where(lane == h, q, 0.0) for h in range(H_BG)], axis=0).astype(bf16)
        kk = jnp.concatenate([k_ref[pl.ds(p0, qb), :], k_ref[pl.ds(q0, qb), :]], axis=0).astype(bf16)
        vv = jnp.concatenate([v_ref[pl.ds(p0, qb), :], v_ref[pl.ds(q0, qb), :]], axis=0).astype(bf16)
        valid = jnp.logical_and(band, jnp.logical_or(col >= qb, i > 0))
        s = jnp.where(valid, _dot_nt(qs, kk), NEG)
        m = jnp.max(s, axis=-1, keepdims=True)
        pr = jnp.exp(s - m)
        den = jnp.sum(pr, axis=-1, keepdims=True)
        o = _dot(pr.astype(bf16), vv) / den
        lse = m + jnp.log(den)
        o_acc = jnp.zeros((qb, B_W), f32)
        l_acc = jnp.zeros((qb, B_W), f32)
        for h in range(H_BG):
            sel = lane == h
            o_acc = jnp.where(sel, o[h * qb:(h + 1) * qb, :], o_acc)
            l_acc = jnp.where(sel, lse[h * qb:(h + 1) * qb, :], l_acc)
        o_ref[pl.ds(q0, qb), :] = o_acc
        lse_ref[pl.ds(q0, qb), :] = l_acc
        return carry

    lax.fori_loop(0, nblk, blk, 0)


def _dil_prompt(p, cols, g, nb, t):
    window, dil = DIL_PAIRS[g]
    assert window // dil == LANE and t % (dil * LANE) == 0
    n, width = p.shape
    ts = t // dil
    qb = LANE
    ncol = width // B_W
    pv = p.reshape(n // dil, dil * width)
    spec = lambda c: pl.BlockSpec((ts, B_W), lambda b, r: (b, r * ncol + c // B_W))
    o, lse = pl.pallas_call(
        functools.partial(_dil_prompt_body, nblk=ts // qb, qb=qb),
        grid=(nb, dil),
        in_specs=[spec(cols[0]), spec(cols[1]), spec(cols[2])],
        out_specs=[pl.BlockSpec((ts, B_W), lambda b, r: (b, r))] * 2,
        out_shape=[jax.ShapeDtypeStruct((n // dil, dil * B_W), f32)] * 2,
        compiler_params=_cp(("arbitrary", "arbitrary"), VMEM_LIMIT),
        name=f"dil_prompt_g{g}",
    )(pv, pv, pv)
    return o.reshape(n, B_W), lse.reshape(n, B_W)


def _dil_sample_body(*refs, clen, window, dil, dt):
    c_ref, q_ref, kn_ref, vn_ref = refs[:4]
    oc_ref, o_ref, lse_ref = refs[-3:]
    kn = kn_ref[...]
    vn = vn_ref[...]
    oc_ref[:, 0:clen - dt] = c_ref[:, dt:clen]
    oc_ref[:, clen - dt:clen] = jnp.concatenate([kn, vn], axis=1).T

    lane = _head_lane(B_W)
    q = q_ref[...] * (DH_B ** -0.5)
    qs = jnp.concatenate([jnp.where(lane == h, q, 0.0) for h in range(H_BG)], axis=0).astype(bf16)
    nq = H_BG * dt
    kc = c_ref[0:B_W, :].astype(bf16)
    vc = c_ref[B_W:2 * B_W, :].astype(bf16)
    qi_c = jnp.bitwise_and(lax.broadcasted_iota(jnp.int32, (nq, clen), 0), dt - 1)
    dist_c = clen + qi_c - lax.broadcasted_iota(jnp.int32, (nq, clen), 1)
    ok_c = jnp.logical_and(jnp.bitwise_and(dist_c, dil - 1) == 0, dist_c <= window)
    qi_n = jnp.bitwise_and(lax.broadcasted_iota(jnp.int32, (nq, dt), 0), dt - 1)
    dist_n = qi_n - lax.broadcasted_iota(jnp.int32, (nq, dt), 1)
    ok_n = jnp.logical_and(dist_n >= 0, jnp.bitwise_and(dist_n, dil - 1) == 0)
    sc = jnp.where(ok_c, _dot(qs, kc), NEG)
    sn = jnp.where(ok_n, _dot_nt(qs, kn.astype(bf16)), NEG)
    m = jnp.maximum(jnp.max(sc, axis=-1, keepdims=True), jnp.max(sn, axis=-1, keepdims=True))
    pc = jnp.exp(sc - m)
    pn = jnp.exp(sn - m)
    den = jnp.sum(pc, axis=-1, keepdims=True) + jnp.sum(pn, axis=-1, keepdims=True)
    num = _dot_nt(pc.astype(bf16), vc) + _dot(pn.astype(bf16), vn.astype(bf16))
    o = num / den
    lse = m + jnp.log(den)
    o_out = jnp.zeros((dt, B_W), f32)
    l_out = jnp.zeros((dt, B_W), f32)
    for h in range(H_BG):
        sel = lane == h
        o_out = jnp.where(sel, o[h * dt:(h + 1) * dt, :], o_out)
        l_out = jnp.where(sel, jnp.broadcast_to(lse[h * dt:(h + 1) * dt, :], (dt, B_W)), l_out)
    o_ref[...] = o_out
    lse_ref[...] = l_out


def _dil_sample(p, cache_t, cache_out, o_prev, l_prev, g, i, db, dt, row0):
    window, dil = DIL_PAIRS[g]
    n_even, _, feat, clen = cache_t.shape
    n = p.shape[0]
    rb0 = row0 // dt
    rows = lambda cb: pl.BlockSpec((dt, B_W), lambda b: (rb0 + b, cb))
    any_spec = pl.BlockSpec(memory_space=pl.ANY)
    args = [cache_t, p, p, p, o_prev, l_prev]
    in_specs = [
        pl.BlockSpec((None, None, feat, clen), lambda b: (i, b, 0, 0)),
        rows(COL_QB // B_W + g), rows(COL_KB // B_W + g), rows(COL_VB // B_W + g),
        any_spec, any_spec,
    ]
    aliases = {4: 1, 5: 2}
    if cache_out is not None:
        args.append(cache_out)
        in_specs.append(any_spec)
        aliases[6] = 0
    return pl.pallas_call(
        functools.partial(_dil_sample_body, clen=clen, window=window, dil=dil, dt=dt),
        grid=(db,),
        in_specs=in_specs,
        out_specs=[
            pl.BlockSpec((None, None, feat, clen), lambda b: (i, b, 0, 0)),
            pl.BlockSpec((dt, B_W), lambda b: (rb0 + b, 0)),
            pl.BlockSpec((dt, B_W), lambda b: (rb0 + b, 0)),
        ],
        out_shape=[
            jax.ShapeDtypeStruct((n_even, db, feat, clen), f32),
            jax.ShapeDtypeStruct((n, B_W), f32),
            jax.ShapeDtypeStruct((n, B_W), f32),
        ],
        input_output_aliases=aliases,
        compiler_params=_cp(("arbitrary",), VMEM_LIMIT),
        name=f"dil_sample_g{g}",
    )(*args)


def _sb_local(z, valid, umat, bits=False):
    zpos = jnp.maximum(z, 0.0)
    zneg = jnp.minimum(z, 0.0)
    if bits:
        lg = jnp.log(1.0 + jnp.exp2(zneg - zpos)) * LOG2E
    else:
        lg = jnp.log(1.0 + jnp.exp(zneg - zpos))
    sp = zpos + lg
    if valid is not None:
        sp = jnp.where(valid, sp, 0.0)
    expo = zneg - lg - _dot(sp.astype(bf16), umat)
    return expo, jnp.sum(sp, axis=-1, keepdims=True)


def _sb_finish(expo, c, valid, bits=False):
    w = jnp.exp2(expo - c) if bits else jnp.exp(expo - c)
    if valid is not None:
        w = jnp.where(valid, w, 0.0)
    return w.astype(bf16)


def _sb_weights(z, c, valid, umat):
    expo, sps = _sb_local(z, valid, umat)
    return _sb_finish(expo, c, valid), sps


def _later_key_matrix(tk):
    return (lax.broadcasted_iota(jnp.int32, (tk, tk), 0) > lax.broadcasted_iota(jnp.int32, (tk, tk), 1)).astype(bf16)


def _sb_prompt_body(bias_ref, q_ref, kt_ref, vt_ref, o_ref, kb_ref, vb_ref, *, tq, tk, nblk):
    hp = pl.program_id(1)
    qi = pl.program_id(2)
    frow = lax.broadcasted_iota(jnp.int32, (2 * DH_C, 1), 0)

    def bias_slot(hh):
        return (1 - hh) * DH_C

    @pl.when(qi == 0)
    def _():
        for j in range(nblk):
            kj = kt_ref[:, j * tk:(j + 1) * tk]
            vj = vt_ref[:, j * tk:(j + 1) * tk]
            for hh in range(2):
                ones = jnp.logical_or(frow == bias_slot(hh), frow == bias_slot(hh) + 1)
                kb_ref[hh, j] = jnp.where(ones, 1.0, kj).astype(bf16)
                vb_ref[hh, j] = jnp.where(frow // DH_C == hh, vj, 0.0).astype(bf16)

    lane = lax.broadcasted_iota(jnp.int32, (1, 2 * DH_C), 1)
    q = q_ref[...] * (DH_C ** -0.5 * LOG2E)
    qh = []
    for hh in range(2):
        b = jnp.full((1, 2 * DH_C), bias_ref[2 * hp + hh] * LOG2E, f32)
        b_hi = b.astype(bf16).astype(f32)
        tail = jnp.where(lane == bias_slot(hh), b_hi, jnp.where(lane == bias_slot(hh) + 1, b - b_hi, 0.0))
        qh.append(jnp.where(lane // DH_C == hh, q, tail).astype(bf16))
    umat = _later_key_matrix(tk)
    ahead = lax.broadcasted_iota(jnp.int32, (tq, tk), 1) - lax.broadcasted_iota(jnp.int32, (tq, tk), 0)

    def pair(s, carry, masked):
        cs, acc = carry
        new_cs = []
        for hh in range(2):
            parts = []
            for kb in (2 * s + 1, 2 * s):
                valid = (ahead + (kb * tk - qi * tq)) < 0 if masked else None
                expo, sps = _sb_local(_dot(qh[hh], kb_ref[hh, kb]), valid, umat, bits=True)
                parts.append((kb, expo, sps, valid))
            c = cs[hh]
            for kb, expo, sps, valid in parts:
                acc = acc + _dot_nt(_sb_finish(expo, c, valid, bits=True), vb_ref[hh, kb])
                c = c + sps
            new_cs.append(c)
        return tuple(new_cs), acc

    zero_c = jnp.zeros((tq, 1), f32)
    sd = lax.div(qi * tq, 2 * tk)
    carry = pair(sd, ((zero_c, zero_c), jnp.zeros((tq, 2 * DH_C), f32)), True)
    _, acc = lax.fori_loop(0, sd, lambda s, c: pair(sd - 1 - s, c, False), carry)
    o_ref[...] = acc


def _sb_prompt(q_rows, kt, vt, bias, i, nb, t):
    n = q_rows.shape[0]
    tq, tk = SB_TQ, SB_TK
    assert t % (2 * tk) == 0 and (2 * tk) % tq == 0
    nq = t // tq
    nblk = t // tk
    nhp = H_C // 2
    w = 2 * DH_C
    return pl.pallas_call(
        functools.partial(_sb_prompt_body, tq=tq, tk=tk, nblk=nblk),
        grid_spec=pltpu.PrefetchScalarGridSpec(
            num_scalar_prefetch=1,
            grid=(nb, nhp, nq),
            in_specs=[
                pl.BlockSpec((tq, w), lambda b, hp, qi, bias: (b * nq + qi, hp)),
                pl.BlockSpec((None, None, w, t), lambda b, hp, qi, bias: (i, b, hp, 0)),
                pl.BlockSpec((None, None, w, t), lambda b, hp, qi, bias: (i, b, hp, 0)),
            ],
            out_specs=pl.BlockSpec((tq, w), lambda b, hp, qi, bias: (b * nq + qi, hp)),
            scratch_shapes=[
                pltpu.VMEM((2, nblk, w, tk), bf16),
                pltpu.VMEM((2, nblk, w, tk), bf16),
            ],
        ),
        out_shape=jax.ShapeDtypeStruct((n, ODD_W), f32),
        compiler_params=_cp(("arbitrary", "arbitrary", "arbitrary"), VMEM_LIMIT),
        name="sb_prompt",
    )(bias, q_rows, kt, vt)


def _sb_sample_body(*refs, dt, nstep, group):
    q_ref, kvn_ref = refs[1:3]
    kc_refs = refs[3:3 + group]
    vc_refs = refs[3 + group:3 + 2 * group]
    bias_ref = refs[3 + 2 * group]
    o_ref, qbd_ref, acc_ref, c_ref = refs[-4:]
    p = pl.program_id(1)
    nrow = H_C * dt
    umat = _later_key_matrix(PAGE_SIZE)
    bias = bias_ref[...]

    @pl.when(p == 0)
    def _():
        lane = lax.broadcasted_iota(jnp.int32, (1, ODD_W), 1) // DH_C
        q = q_ref[...] * (DH_C ** -0.5)
        qbd = jnp.concatenate([jnp.where(lane == h, q, 0.0) for h in range(H_C)], axis=0).astype(bf16)
        qbd_ref[...] = qbd
        pad = jnp.zeros((PAGE_SIZE - dt, ODD_W), f32)
        kblk = jnp.concatenate([kvn_ref[:, 0:ODD_W], pad], axis=0).astype(bf16)
        vblk = jnp.concatenate([kvn_ref[:, ODD_W:2 * ODD_W], pad], axis=0).astype(bf16)
        qidx = jnp.bitwise_and(lax.broadcasted_iota(jnp.int32, (nrow, PAGE_SIZE), 0), dt - 1)
        valid = lax.broadcasted_iota(jnp.int32, (nrow, PAGE_SIZE), 1) < qidx
        w, sps = _sb_weights(_dot_nt(qbd, kblk) + bias, jnp.zeros((nrow, 1), f32), valid, umat)
        acc_ref[...] = _dot(w, vblk)
        c_ref[...] = sps

    @pl.when(p > 0)
    def _():
        qbd = qbd_ref[...]
        c = c_ref[...]
        ws = []
        for j in range(group):
            kt = kc_refs[j][...].reshape(ODD_W, PAGE_SIZE).astype(bf16)
            w, sps = _sb_weights(_dot(qbd, kt) + bias, c, None, umat)
            ws.append(w)
            c = c + sps
        c_ref[...] = c
        vt = jnp.concatenate([vc_refs[j][...].reshape(ODD_W, PAGE_SIZE).astype(bf16) for j in range(group)], axis=1)
        acc_ref[...] += _dot_nt(jnp.concatenate(ws, axis=1), vt)

    @pl.when(p == nstep)
    def _():
        lane = lax.broadcasted_iota(jnp.int32, (1, ODD_W), 1) // DH_C
        out = jnp.zeros((dt, ODD_W), f32)
        for h in range(H_C):
            out = jnp.where(lane == h, acc_ref[h * dt:(h + 1) * dt, :], out)
        o_ref[...] = out


def _sb_sample(q_rows, kv_new, cache_k, cache_v, page_table, bias_col, o_prev, i, db, dt, row0):
    n = q_rows.shape[0]
    npage = page_table.shape[1]
    group = 8 if npage % 8 == 0 else 1
    nstep = npage // group
    kc = jnp.transpose(cache_k, (0, 1, 3, 4, 2))
    vc = jnp.transpose(cache_v, (0, 1, 3, 4, 2))
    rb0 = row0 // dt
    nrow = H_C * dt

    def page(j):
        return lambda b, p, pt: (i, pt[b, npage - 1 - (jnp.maximum(p, 1) - 1) * group - j], 0, 0, 0)

    page_specs = [pl.BlockSpec((None, None, H_C, DH_C, PAGE_SIZE), page(j)) for j in range(group)]
    return pl.pallas_call(
        functools.partial(_sb_sample_body, dt=dt, nstep=nstep, group=group),
        grid_spec=pltpu.PrefetchScalarGridSpec(
            num_scalar_prefetch=1,
            grid=(db, nstep + 1),
            in_specs=[
                pl.BlockSpec((dt, ODD_W), lambda b, p, pt: (rb0 + b, 0)),
                pl.BlockSpec((dt, 2 * ODD_W), lambda b, p, pt: (b, 0)),
            ] + page_specs + page_specs + [
                pl.BlockSpec((nrow, 1), lambda b, p, pt: (0, 0)),
                pl.BlockSpec(memory_space=pl.ANY),
            ],
            out_specs=pl.BlockSpec((dt, ODD_W), lambda b, p, pt: (rb0 + b, 0)),
            scratch_shapes=[
                pltpu.VMEM((nrow, ODD_W), bf16),
                pltpu.VMEM((nrow, ODD_W), f32),
                pltpu.VMEM((nrow, 1), f32),
            ],
        ),
        out_shape=jax.ShapeDtypeStruct((n, ODD_W), f32),
        input_output_aliases={3 + 2 * group + 1: 0},
        compiler_params=_cp(("arbitrary", "arbitrary"), VMEM_LIMIT),
        name="sb_sample",
    )(page_table, q_rows, kv_new, *([kc] * group), *([vc] * group), bias_col, o_prev)


def _last_rows(a, n):
    t = a.shape[1]
    if t >= n:
        return a[:, t - n:]
    return jnp.pad(a, [(0, 0), (n - t, 0)] + [(0, 0)] * (a.ndim - 2))


def _token_major(a):
    n_odd, nb, _, t = a.shape
    return jnp.transpose(a.reshape(n_odd, nb, H_C, DH_C, t), (0, 1, 4, 2, 3))


def _win_rows(c):
    n_even, db, _, clen = c.shape
    return jnp.transpose(c.reshape(n_even, db, 2, H_BG, DH_B, clen), (0, 1, 5, 2, 3, 4))


def kernel(x_prompt, x_sample, state_gla, cache_win_g0, cache_win_g1, cache_win_g2, cache_sb_k, cache_sb_v, page_table, norm_pre, norm_post, ffn_w_gate, ffn_w_up, ffn_w_down, even_w_in, gla_w_gate, gla_b_gate, gla_norm, even_w_out, odd_w_in, odd_w_out, sb_bias):
    nb, t, d = x_prompt.shape
    db, dt, _ = x_sample.shape
    depth = norm_pre.shape[0]
    n_p, n_s = nb * t, db * dt
    win_caches = tuple(jnp.transpose(c, (0, 1, 3, 4, 5, 2)).reshape(c.shape[0], db, 2 * B_W, c.shape[2])
                       for c in (cache_win_g0, cache_win_g1, cache_win_g2))
    assert dt == SUBLANE and n_p % (16 * SUBLANE) == 0 and n_s % 16 == 0

    wg = ffn_w_gate.astype(bf16)
    wu = ffn_w_up.astype(bf16)
    wd = ffn_w_down.astype(bf16)
    g0, g1 = COL_QB, COL_QB + GATE_RANK
    w_main = jnp.concatenate([even_w_in[:, :, :g0], even_w_in[:, :, g1:]], axis=2).astype(bf16)
    w_gate_in = jnp.pad(even_w_in[:, :, g0:g1], ((0, 0), (0, 0), (0, LANE - GATE_RANK))).astype(bf16)
    gla_gw = jnp.pad(gla_w_gate, ((0, 0), (0, LANE - GATE_RANK), (0, 0))).astype(bf16)
    gla_gb = gla_b_gate[:, None, :]
    gla_gn = gla_norm[:, None, :]
    w_out_e = even_w_out.astype(bf16)
    w_in_o = odd_w_in.astype(bf16)
    w_out_o = odd_w_out.astype(bf16)
    npre = norm_pre[:, :, None, :]
    npost = norm_post[:, :, None, :]
    bias_cols = jnp.repeat(sb_bias, dt, axis=1)[:, :, None]

    x = (x_prompt.reshape(n_p, d), x_sample.reshape(n_s, d))
    c_p = GLA_CHUNK if t % GLA_CHUNK == 0 else t
    c_s = GLA_CHUNK if dt % GLA_CHUNK == 0 else dt

    gla_p, gla_s = [], []
    win_p, win_s = [[] for _ in DIL_PAIRS], [None for _ in DIL_PAIRS]
    kt_all, vt_all, sbk_s, sbv_s = None, None, [], []
    for l in range(depth):
        i = l // 2
        x = _ffn(x, npre, wg, wu, wd, npost, l, 0, n_p)
        if l % 2 == 0:
            p, la, *p_grp = _proj_even(x, npre, w_main, w_gate_in, gla_gw, gla_gb, l, i)
            oa_p, sp_new = _gla(p, la, jnp.zeros((nb, H_A, DK_A, DV_A), f32), gla_gn, i, nb, t // c_p, c_p, 0,
                                math.gcd(nb, GLA_SEQS_PER_STEP))
            oa_s, ss_new = _gla(p, la, state_gla[i], gla_gn, i, db, dt // c_s, c_s, n_p,
                                math.gcd(db, 2 * GLA_SEQS_PER_STEP))
            gla_p.append(sp_new)
            gla_s.append(ss_new)
            og, lg = [], []
            for g, (window, _) in enumerate(DIL_PAIRS):
                if g == 0:
                    o_g, l_g = _dil_prompt(p, (COL_QB, COL_KB, COL_VB), g, nb, t)
                else:
                    o_g, l_g = _dil_prompt(p_grp[g - 1], (0, B_W, 2 * B_W), g, nb, t)
                win_s[g], o_g, l_g = _dil_sample(p, win_caches[g], win_s[g], o_g, l_g, g, i, db, dt, n_p)
                og.append(o_g)
                lg.append(l_g)
                kcol, vcol = COL_KB + g * B_W, COL_VB + g * B_W
                kv = jnp.stack([p[:n_p, kcol:kcol + B_W].reshape(nb, t, H_BG, DH_B),
                                p[:n_p, vcol:vcol + B_W].reshape(nb, t, H_BG, DH_B)], axis=2)
                win_p[g].append(_last_rows(kv, min(window, page_table.shape[1] * PAGE_SIZE)))
            x = _post_even(x, (oa_p, oa_s), og, lg, w_out_e, npost, l, i)
        else:
            q_rows, kt_all, vt_all, kv_new = _proj_odd(x, npre, w_in_o, l, i, nb, t, kt_all, vt_all)
            o = _sb_prompt(q_rows, kt_all, vt_all, sb_bias[i], i, nb, t)
            o = _sb_sample(q_rows, kv_new, cache_sb_k, cache_sb_v, page_table, bias_cols[i], o, i, db, dt, n_p)
            sbk_s.append(kv_new[:, :ODD_W].reshape(db, dt, H_C, DH_C))
            sbv_s.append(kv_new[:, ODD_W:].reshape(db, dt, H_C, DH_C))
            x = _post_odd(x, o, w_out_o, npost, l, i)
        x = _ffn(x, npre, wg, wu, wd, npost, l, 1, n_p, split_out=(l == depth - 1))

    return (x[0].reshape(nb, t, d), x[1].reshape(db, dt, d),
            jnp.stack(gla_p), jnp.stack(gla_s),
            jnp.stack(win_p[0]), jnp.stack(win_p[1]), jnp.stack(win_p[2]),
            _win_rows(win_s[0]), _win_rows(win_s[1]), _win_rows(win_s[2]),
            _token_major(kt_all), _token_major(vt_all), jnp.stack(sbk_s), jnp.stack(sbv_s))
```

```python
import functools
import math

import jax
import jax.numpy as jnp
from jax import lax
from jax.experimental import pallas as pl
from jax.experimental.pallas import tpu as pltpu

f32 = jnp.float32
bf16 = jnp.bfloat16

EPS = 1e-6
H_A, DK_A, DV_A = 4, 64, 128
GATE_RANK = 16
GATE_NORM = 16.0
GLA_CHUNK = 64
DIL_PAIRS = ((128, 1), (512, 4), (2048, 16))
N_DIL = 3
H_BG, DH_B = 4, 64
H_C, DH_C = 16, 64
PAGE_SIZE = 128

A_Q, A_K, A_V, A_R = H_A * DK_A, H_A * DK_A, H_A * DV_A, H_A * DV_A
B_W = H_BG * DH_B
EVEN_MAIN = A_Q + A_K + A_V + A_R + 3 * N_DIL * B_W
COL_QB = A_Q + A_K + A_V + A_R
COL_KB = COL_QB + N_DIL * B_W
COL_VB = COL_KB + N_DIL * B_W
ODD_W = H_C * DH_C

LANE = 128
SUBLANE = 8
NEG = -1e30
LOG2E = 1.4426950408889634
VMEM_LIMIT = 56 * 1024 * 1024
GLA_SEQS_PER_STEP = 4
SB_CHAIN_DTYPE = f32
SB_PAGES_PER_STEP = 16
SB_TQ, SB_TK = 512, 256


def _cp(sem, vmem=None):
    return pltpu.CompilerParams(dimension_semantics=sem, vmem_limit_bytes=vmem)


def _rms(x, g):
    return x * lax.rsqrt(jnp.mean(x * x, axis=-1, keepdims=True) + EPS) * g


def _dot(a, b):
    return jnp.dot(a, b, preferred_element_type=f32)


def _dot_nt(a, b):
    return lax.dot_general(a, b, (((1,), (1,)), ((), ())), preferred_element_type=f32)


def _dot_tn(a, b):
    return lax.dot_general(a, b, (((0,), (0,)), ((), ())), preferred_element_type=f32)


def _row_tile(n):
    for t in (256, 128, 64, 32, 16, 8):
        if n % t == 0:
            return t
    raise ValueError(f"row count {n} is not a multiple of {SUBLANE}")


def _ffn_body(*refs, n_ptiles, split_in, split_out):
    r = pl.program_id(0)
    n_in = 2 if split_in else 1
    gpre_ref, wg_ref, wu_ref, wd_ref, gpost_ref = refs[n_in:n_in + 5]
    outs = refs[n_in + 5:]
    x = jnp.where(r < n_ptiles, refs[0][...], refs[1][...]) if split_in else refs[0][...]
    h = _rms(x, gpre_ref[...]).astype(bf16)
    gate = _dot(h, wg_ref[...])
    up = _dot(h, wu_ref[...])
    a = (gate * jax.nn.sigmoid(gate) * up).astype(bf16)
    y = x + 0.5 * _rms(_dot(a, wd_ref[...]), gpost_ref[...])
    if split_out:
        @pl.when(r < n_ptiles)
        def _():
            outs[0][...] = y

        @pl.when(r >= n_ptiles)
        def _():
            outs[1][...] = y
    else:
        outs[0][...] = y


def _ffn(xs, gpre, wg, wu, wd, gpost, l, j, n_p, split_out=False):
    split_in = isinstance(xs, tuple)
    xs = xs if split_in else (xs,)
    n = sum(a.shape[0] for a in xs)
    d = xs[0].shape[1]
    ff = wg.shape[-1]
    tm = _row_tile(n)
    assert n_p % tm == 0
    n_ptiles = n_p // tm
    once = pl.Buffered(1)
    prompt_rows = pl.BlockSpec((tm, d), lambda r: (jnp.minimum(r, n_ptiles - 1), 0))
    sample_rows = pl.BlockSpec((tm, d), lambda r: (jnp.maximum(r - n_ptiles, 0), 0))
    all_rows = pl.BlockSpec((tm, d), lambda r: (r, 0))
    if split_out:
        out_specs = [prompt_rows, sample_rows]
        out_shape = [jax.ShapeDtypeStruct((n_p, d), f32), jax.ShapeDtypeStruct((n - n_p, d), f32)]
    else:
        out_specs, out_shape = all_rows, jax.ShapeDtypeStruct((n, d), f32)
    return pl.pallas_call(
        functools.partial(_ffn_body, n_ptiles=n_ptiles, split_in=split_in, split_out=split_out),
        grid=(n // tm,),
        in_specs=([prompt_rows, sample_rows] if split_in else [all_rows]) + [
            pl.BlockSpec((None, None, 1, d), lambda r: (l, 2 * j, 0, 0)),
            pl.BlockSpec((None, None, d, ff), lambda r: (l, j, 0, 0), pipeline_mode=once),
            pl.BlockSpec((None, None, d, ff), lambda r: (l, j, 0, 0), pipeline_mode=once),
            pl.BlockSpec((None, None, ff, d), lambda r: (l, j, 0, 0), pipeline_mode=once),
            pl.BlockSpec((None, None, 1, d), lambda r: (l, 2 * j, 0, 0)),
        ],
        out_specs=out_specs,
        out_shape=out_shape,
        compiler_params=_cp(("arbitrary",), VMEM_LIMIT),
        name="ffn",
    )(*xs, gpre, wg, wu, wd, gpost)


def _proj_even_body(x_ref, g_ref, wm_ref, wgate_ref, gw_ref, gb_ref, p_ref, la_ref, *grp_refs):
    h = _rms(x_ref[...], g_ref[...]).astype(bf16)
    p = _dot(h, wm_ref[...])
    p_ref[...] = p
    for g, ref in enumerate(grp_refs, start=1):
        ref[...] = jnp.concatenate([p[:, c + g * B_W:c + (g + 1) * B_W] for c in (COL_QB, COL_KB, COL_VB)], axis=1)
    ga = _dot(h, wgate_ref[...])
    lg = _dot(ga.astype(bf16), gw_ref[...]) + gb_ref[...]
    la_ref[...] = (jnp.minimum(lg, 0.0) - jnp.log1p(jnp.exp(-jnp.abs(lg)))) / GATE_NORM


def _proj_even(x, norm_pre, wm, wgate, gw, gb, l, i):
    n, d = x.shape
    tm = _row_tile(n)
    once = pl.Buffered(1)
    return pl.pallas_call(
        _proj_even_body,
        grid=(n // tm,),
        in_specs=[
            pl.BlockSpec((tm, d), lambda r: (r, 0)),
            pl.BlockSpec((None, None, 1, d), lambda r: (l, 1, 0, 0)),
            pl.BlockSpec((None, d, EVEN_MAIN), lambda r: (i, 0, 0), pipeline_mode=once),
            pl.BlockSpec((None, d, LANE), lambda r: (i, 0, 0), pipeline_mode=once),
            pl.BlockSpec((None, LANE, A_K), lambda r: (i, 0, 0), pipeline_mode=once),
            pl.BlockSpec((None, 1, A_K), lambda r: (i, 0, 0)),
        ],
        out_specs=[
            pl.BlockSpec((tm, EVEN_MAIN), lambda r: (r, 0)),
            pl.BlockSpec((tm, A_K), lambda r: (r, 0)),
        ] + [pl.BlockSpec((tm, 3 * B_W), lambda r: (r, 0))] * (N_DIL - 1),
        out_shape=[
            jax.ShapeDtypeStruct((n, EVEN_MAIN), f32),
            jax.ShapeDtypeStruct((n, A_K), f32),
        ] + [jax.ShapeDtypeStruct((n, 3 * B_W), f32)] * (N_DIL - 1),
        compiler_params=_cp(("arbitrary",), VMEM_LIMIT),
        name="proj_even",
    )(x, norm_pre, wm, wgate, gw, gb)


def _proj_odd_body(*refs, n_ptiles):
    x_ref, g_ref, w_ref = refs[:3]
    q_ref, kt_ref, vt_ref, kvs_ref = refs[-4:]
    r = pl.program_id(0)
    h = _rms(x_ref[...], g_ref[...]).astype(bf16)
    p = _dot(h, w_ref[...])
    q_ref[...] = p[:, :ODD_W]

    @pl.when(r < n_ptiles)
    def _():
        kt_ref[...] = p[:, ODD_W:2 * ODD_W].T
        vt_ref[...] = p[:, 2 * ODD_W:].T

    @pl.when(r >= n_ptiles)
    def _():
        kvs_ref[...] = p[:, ODD_W:]


def _proj_odd(x, norm_pre, w, l, i, nb, t, kt_prev, vt_prev):
    n, d = x.shape
    tm = _row_tile(n)
    assert t % tm == 0
    n_odd = w.shape[0]
    tpb = t // tm
    n_ptiles = nb * tpb
    n_s = n - nb * t

    def kt_map(r):
        rp = jnp.minimum(r, n_ptiles - 1)
        return (i, rp // tpb, 0, rp % tpb)

    args = [x, norm_pre, w]
    in_specs = [
        pl.BlockSpec((tm, d), lambda r: (r, 0)),
        pl.BlockSpec((None, None, 1, d), lambda r: (l, 1, 0, 0)),
        pl.BlockSpec((None, d, 3 * ODD_W), lambda r: (i, 0, 0), pipeline_mode=pl.Buffered(1)),
    ]
    aliases = {}
    if kt_prev is not None:
        args += [kt_prev, vt_prev]
        in_specs += [pl.BlockSpec(memory_space=pl.ANY)] * 2
        aliases = {3: 1, 4: 2}
    stack = jax.ShapeDtypeStruct((n_odd, nb, ODD_W, t), f32)
    return pl.pallas_call(
        functools.partial(_proj_odd_body, n_ptiles=n_ptiles),
        grid=(n // tm,),
        in_specs=in_specs,
        out_specs=[
            pl.BlockSpec((tm, ODD_W), lambda r: (r, 0)),
            pl.BlockSpec((None, None, ODD_W, tm), kt_map),
            pl.BlockSpec((None, None, ODD_W, tm), kt_map),
            pl.BlockSpec((tm, 2 * ODD_W), lambda r: (jnp.maximum(r - n_ptiles, 0), 0)),
        ],
        out_shape=[jax.ShapeDtypeStruct((n, ODD_W), f32), stack, stack,
                   jax.ShapeDtypeStruct((n_s, 2 * ODD_W), f32)],
        input_output_aliases=aliases,
        compiler_params=_cp(("arbitrary",), VMEM_LIMIT),
        name="proj_odd",
    )(*args)


def _post_even_body(x_ref, oap_ref, oas_ref, o0_ref, o1_ref, o2_ref, l0_ref, l1_ref, l2_ref, w_ref, g_ref, y_ref,
                    *, n_ptiles):
    oa = jnp.where(pl.program_id(0) < n_ptiles, oap_ref[...], oas_ref[...])
    l0, l1, l2 = l0_ref[...], l1_ref[...], l2_ref[...]
    mx = jnp.maximum(jnp.maximum(l0, l1), l2)
    e0, e1, e2 = jnp.exp(l0 - mx), jnp.exp(l1 - mx), jnp.exp(l2 - mx)
    den = e0 + e1 + e2
    ob = (e0 / den) * o0_ref[...] + (e1 / den) * o1_ref[...] + (e2 / den) * o2_ref[...]
    m = _dot(oa.astype(bf16), w_ref[0:A_V, :]) + _dot(ob.astype(bf16), w_ref[A_V:A_V + B_W, :])
    y_ref[...] = x_ref[...] + _rms(m, g_ref[...])


def _post_even(x, oa, og, lg, w_out, norm_post, l, i):
    n, d = x.shape
    tm = _row_tile(n)
    n_p = oa[0].shape[0]
    assert n_p % tm == 0 and oa[1].shape[0] % tm == 0
    n_ptiles = n_p // tm
    row = lambda w: pl.BlockSpec((tm, w), lambda r: (r, 0))
    oa_specs = [pl.BlockSpec((tm, A_V), lambda r: (jnp.minimum(r, n_ptiles - 1), 0)),
                pl.BlockSpec((tm, A_V), lambda r: (jnp.maximum(r - n_ptiles, 0), 0))]
    return pl.pallas_call(
        functools.partial(_post_even_body, n_ptiles=n_ptiles),
        grid=(n // tm,),
        in_specs=[row(d)] + oa_specs + [row(B_W)] * 6 + [
            pl.BlockSpec((None, A_V + B_W, d), lambda r: (i, 0, 0), pipeline_mode=pl.Buffered(1)),
            pl.BlockSpec((None, None, 1, d), lambda r: (l, 1, 0, 0)),
        ],
        out_specs=row(d),
        out_shape=jax.ShapeDtypeStruct((n, d), f32),
        compiler_params=_cp(("arbitrary",), VMEM_LIMIT),
        name="post_even",
    )(x, oa[0], oa[1], og[0], og[1], og[2], lg[0], lg[1], lg[2], w_out, norm_post)


def _post_odd_body(x_ref, o_ref, w_ref, g_ref, y_ref):
    m = _dot(o_ref[...].astype(bf16), w_ref[...])
    y_ref[...] = x_ref[...] + _rms(m, g_ref[...])


def _post_odd(x, o, w_out, norm_post, l, i):
    n, d = x.shape
    tm = _row_tile(n)
    row = lambda w: pl.BlockSpec((tm, w), lambda r: (r, 0))
    return pl.pallas_call(
        _post_odd_body,
        grid=(n // tm,),
        in_specs=[row(d), row(ODD_W),
                  pl.BlockSpec((None, ODD_W, d), lambda r: (i, 0, 0), pipeline_mode=pl.Buffered(1)),
                  pl.BlockSpec((None, None, 1, d), lambda r: (l, 1, 0, 0))],
        out_specs=row(d),
        out_shape=jax.ShapeDtypeStruct((n, d), f32),
        compiler_params=_cp(("arbitrary",), VMEM_LIMIT),
        name="post_odd",
    )(x, o, w_out, norm_post)


def _gla_body(*refs, c, bpg, nchunk):
    s0_ref, gn_ref, o_ref, s_ref, sbd_ref = refs[5 * bpg:]
    n = pl.program_id(1)
    dk_all, dv_all = H_A * DK_A, H_A * DV_A
    on_diag = (lax.broadcasted_iota(jnp.int32, (dk_all, dv_all), 0) // DK_A
               == lax.broadcasted_iota(jnp.int32, (dk_all, dv_all), 1) // DV_A)

    @pl.when(n == 0)
    def _():
        for j in range(bpg):
            rows = [jnp.concatenate([s0_ref[j, h] if g == h else jnp.zeros((DK_A, DV_A), f32) for g in range(H_A)],
                                    axis=1) for h in range(H_A)]
            sbd_ref[j] = jnp.concatenate(rows, axis=0)

    causal = (lax.broadcasted_iota(jnp.int32, (c, c), 1) <= lax.broadcasted_iota(jnp.int32, (c, c), 0))
    ltri = causal.astype(bf16)
    causal_h = (lax.broadcasted_iota(jnp.int32, (H_A * c, c), 1)
                <= jnp.bitwise_and(lax.broadcasted_iota(jnp.int32, (H_A * c, c), 0), c - 1))
    eye = (lax.broadcasted_iota(jnp.int32, (dk_all, dk_all), 0) == lax.broadcasted_iota(jnp.int32, (dk_all, dk_all), 1))
    khead = lax.broadcasted_iota(jnp.int32, (1, dk_all), 1) // DK_A
    vhead = lax.broadcasted_iota(jnp.int32, (1, dv_all), 1) // DV_A
    gn = gn_ref[...]
    for j in range(bpg):
        q_ref, k_ref, v_ref, r_ref, la_ref = refs[5 * j:5 * j + 5]
        q = q_ref[...] * (DK_A ** -0.5)
        k = k_ref[...]
        v = v_ref[...].astype(bf16)
        r = r_ref[...]
        s_old = sbd_ref[j]
        la = la_ref[...]
        la_hi = la.astype(bf16)
        la_lo = (la - la_hi.astype(f32)).astype(bf16)
        cum = _dot(ltri, la_hi) + _dot(ltri, la_lo)
        mid = cum[c // 2:c // 2 + 1, :]
        last = cum[c - 1:c, :]
        q_in = q * jnp.exp(cum - mid)
        k_in = (k * jnp.exp(mid - cum)).astype(bf16)
        q_st = (q * jnp.exp(cum)).astype(bf16)
        k_st = (k * jnp.exp(last - cum)).astype(bf16)
        q_heads = jnp.concatenate([jnp.where(khead == h, q_in, 0.0) for h in range(H_A)], axis=0).astype(bf16)
        scores = jnp.where(causal_h, _dot_nt(q_heads, k_in), 0.0)
        o_all = _dot(scores.astype(bf16), v)
        o = _dot(q_st, s_old.astype(bf16))
        for h in range(H_A):
            o = o + jnp.where(vhead == h, o_all[h * c:(h + 1) * c, :], 0.0)
        e_col = jnp.sum(jnp.where(eye, jnp.broadcast_to(jnp.exp(last), (dk_all, dk_all)), 0.0), axis=1, keepdims=True)
        sbd_ref[j] = e_col * s_old + jnp.where(on_diag, _dot_tn(k_st, v), 0.0)
        for h in range(H_A):
            sv = slice(h * DV_A, (h + 1) * DV_A)
            rh = r[:, sv]
            o_ref[j, :, sv] = _rms(o[:, sv], gn) * (rh * jax.nn.sigmoid(rh))

    @pl.when(n == nchunk - 1)
    def _():
        for j in range(bpg):
            for h in range(H_A):
                s_ref[j, h] = sbd_ref[j, h * DK_A:(h + 1) * DK_A, h * DV_A:(h + 1) * DV_A]


def _gla(p, la, s0, gn, i, nb, nchunk, c, row0, bpg):
    assert nb % bpg == 0
    rb0 = row0 // c
    in_specs = []
    args = []
    for j in range(bpg):
        rowmap = lambda cb, j=j: (lambda g, t: (rb0 + (g * bpg + j) * nchunk + t, cb))
        in_specs += [
            pl.BlockSpec((c, A_Q), rowmap(0)),
            pl.BlockSpec((c, A_K), rowmap(1)),
            pl.BlockSpec((c, A_V), rowmap(1)),
            pl.BlockSpec((c, A_R), rowmap(2)),
            pl.BlockSpec((c, A_K), rowmap(0)),
        ]
        args += [p, p, p, p, la]
    in_specs += [
        pl.BlockSpec((bpg, H_A, DK_A, DV_A), lambda g, t: (g, 0, 0, 0)),
        pl.BlockSpec((None, 1, DV_A), lambda g, t: (i, 0, 0)),
    ]
    args += [s0, gn]
    o, s_new = pl.pallas_call(
        functools.partial(_gla_body, c=c, bpg=bpg, nchunk=nchunk),
        grid=(nb // bpg, nchunk),
        in_specs=in_specs,
        out_specs=[
            pl.BlockSpec((bpg, c, A_V), lambda g, t: (g, t, 0)),
            pl.BlockSpec((bpg, H_A, DK_A, DV_A), lambda g, t: (g, 0, 0, 0)),
        ],
        out_shape=[
            jax.ShapeDtypeStruct((nb, nchunk * c, A_V), f32),
            jax.ShapeDtypeStruct((nb, H_A, DK_A, DV_A), f32),
        ],
        scratch_shapes=[pltpu.VMEM((bpg, H_A * DK_A, H_A * DV_A), f32)],
        compiler_params=_cp(("arbitrary", "arbitrary"), VMEM_LIMIT),
        name="gla",
    )(*args)
    return o.reshape(nb * nchunk * c, A_V), s_new


def _head_lane(width):
    return lax.broadcasted_iota(jnp.int32, (1, width), 1) // DH_B


def _dil_prompt_body(q_ref, k_ref, v_ref, o_ref, lse_ref, *, nblk, qb):
    row = jnp.bitwise_and(lax.broadcasted_iota(jnp.int32, (H_BG * qb, 2 * qb), 0), qb - 1)
    col = lax.broadcasted_iota(jnp.int32, (H_BG * qb, 2 * qb), 1)
    band = jnp.logical_and(col >= row, col <= row + qb)
    lane = _head_lane(B_W)

    def blk(i, carry):
        q0 = pl.multiple_of(i * qb, qb)
        p0 = pl.multiple_of(jnp.maximum(i - 1, 0) * qb, qb)
        q = q_ref[pl.ds(q0, qb), :] * (DH_B ** -0.5)
        qs = jnp.concatenate([jnp.where(lane == h, q, 0.0) for h in range(H_BG)], axis=0).astype(bf16)
        kk = jnp.concatenate([k_ref[pl.ds(p0, qb), :], k_ref[pl.ds(q0, qb), :]], axis=0).astype(bf16)
        vv = jnp.concatenate([v_ref[pl.ds(p0, qb), :], v_ref[pl.ds(q0, qb), :]], axis=0).astype(bf16)
        valid = jnp.logical_and(band, jnp.logical_or(col >= qb, i > 0))
        s = jnp.where(valid, _dot_nt(qs, kk), NEG)
        m = jnp.max(s, axis=-1, keepdims=True)
        pr = jnp.exp(s - m)
        den = jnp.sum(pr, axis=-1, keepdims=True)
        o = _dot(pr.astype(bf16), vv) / den
        lse = m + jnp.log(den)
        o_acc = jnp.zeros((qb, B_W), f32)
        l_acc = jnp.zeros((qb, B_W), f32)
        for h in range(H_BG):
            sel = lane == h
            o_acc = jnp.where(sel, o[h * qb:(h + 1) * qb, :], o_acc)
            l_acc = jnp.where(sel, lse[h * qb:(h + 1) * qb, :], l_acc)
        o_ref[pl.ds(q0, qb), :] = o_acc
        lse_ref[pl.ds(q0, qb), :] = l_acc
        return carry

    lax.fori_loop(0, nblk, blk, 0)


def _dil_prompt(p, cols, g, nb, t):
    window, dil = DIL_PAIRS[g]
    assert window // dil == LANE and t % (dil * LANE) == 0
    n, width = p.shape
    ts = t // dil
    qb = LANE
    ncol = width // B_W
    pv = p.reshape(n // dil, dil * width)
    spec = lambda c: pl.BlockSpec((ts, B_W), lambda b, r: (b, r * ncol + c // B_W))
    o, lse = pl.pallas_call(
        functools.partial(_dil_prompt_body, nblk=ts // qb, qb=qb),
        grid=(nb, dil),
        in_specs=[spec(cols[0]), spec(cols[1]), spec(cols[2])],
        out_specs=[pl.BlockSpec((ts, B_W), lambda b, r: (b, r))] * 2,
        out_shape=[jax.ShapeDtypeStruct((n // dil, dil * B_W), f32)] * 2,
        compiler_params=_cp(("arbitrary", "arbitrary"), VMEM_LIMIT),
        name=f"dil_prompt_g{g}",
    )(pv, pv, pv)
    return o.reshape(n, B_W), lse.reshape(n, B_W)


def _dil_sample_body(*refs, clen, window, dil, dt):
    c_ref, q_ref, kn_ref, vn_ref = refs[:4]
    oc_ref, o_ref, lse_ref = refs[-3:]
    kn = kn_ref[...]
    vn = vn_ref[...]
    oc_ref[:, 0:clen - dt] = c_ref[:, dt:clen]
    oc_ref[:, clen - dt:clen] = jnp.concatenate([kn, vn], axis=1).T

    lane = _head_lane(B_W)
    q = q_ref[...] * (DH_B ** -0.5)
    qs = jnp.concatenate([jnp.where(lane == h, q, 0.0) for h in range(H_BG)], axis=0).astype(bf16)
    nq = H_BG * dt
    kc = c_ref[0:B_W, :].astype(bf16)
    vc = c_ref[B_W:2 * B_W, :].astype(bf16)
    qi_c = jnp.bitwise_and(lax.broadcasted_iota(jnp.int32, (nq, clen), 0), dt - 1)
    dist_c = clen + qi_c - lax.broadcasted_iota(jnp.int32, (nq, clen), 1)
    ok_c = jnp.logical_and(jnp.bitwise_and(dist_c, dil - 1) == 0, dist_c <= window)
    qi_n = jnp.bitwise_and(lax.broadcasted_iota(jnp.int32, (nq, dt), 0), dt - 1)
    dist_n = qi_n - lax.broadcasted_iota(jnp.int32, (nq, dt), 1)
    ok_n = jnp.logical_and(dist_n >= 0, jnp.bitwise_and(dist_n, dil - 1) == 0)
    sc = jnp.where(ok_c, _dot(qs, kc), NEG)
    sn = jnp.where(ok_n, _dot_nt(qs, kn.astype(bf16)), NEG)
    m = jnp.maximum(jnp.max(sc, axis=-1, keepdims=True), jnp.max(sn, axis=-1, keepdims=True))
    pc = jnp.exp(sc - m)
    pn = jnp.exp(sn - m)
    den = jnp.sum(pc, axis=-1, keepdims=True) + jnp.sum(pn, axis=-1, keepdims=True)
    num = _dot_nt(pc.astype(bf16), vc) + _dot(pn.astype(bf16), vn.astype(bf16))
    o = num / den
    lse = m + jnp.log(den)
    o_out = jnp.zeros((dt, B_W), f32)
    l_out = jnp.zeros((dt, B_W), f32)
    for h in range(H_BG):
        sel = lane == h
        o_out = jnp.where(sel, o[h * dt:(h + 1) * dt, :], o_out)
        l_out = jnp.where(sel, jnp.broadcast_to(lse[h * dt:(h + 1) * dt, :], (dt, B_W)), l_out)
    o_ref[...] = o_out
    lse_ref[...] = l_out


def _dil_sample(p, cache_t, cache_out, o_prev, l_prev, g, i, db, dt, row0):
    window, dil = DIL_PAIRS[g]
    n_even, _, feat, clen = cache_t.shape
    n = p.shape[0]
    rb0 = row0 // dt
    rows = lambda cb: pl.BlockSpec((dt, B_W), lambda b: (rb0 + b, cb))
    any_spec = pl.BlockSpec(memory_space=pl.ANY)
    args = [cache_t, p, p, p, o_prev, l_prev]
    in_specs = [
        pl.BlockSpec((None, None, feat, clen), lambda b: (i, b, 0, 0)),
        rows(COL_QB // B_W + g), rows(COL_KB // B_W + g), rows(COL_VB // B_W + g),
        any_spec, any_spec,
    ]
    aliases = {4: 1, 5: 2}
    if cache_out is not None:
        args.append(cache_out)
        in_specs.append(any_spec)
        aliases[6] = 0
    return pl.pallas_call(
        functools.partial(_dil_sample_body, clen=clen, window=window, dil=dil, dt=dt),
        grid=(db,),
        in_specs=in_specs,
        out_specs=[
            pl.BlockSpec((None, None, feat, clen), lambda b: (i, b, 0, 0)),
            pl.BlockSpec((dt, B_W), lambda b: (rb0 + b, 0)),
            pl.BlockSpec((dt, B_W), lambda b: (rb0 + b, 0)),
        ],
        out_shape=[
            jax.ShapeDtypeStruct((n_even, db, feat, clen), f32),
            jax.ShapeDtypeStruct((n, B_W), f32),
            jax.ShapeDtypeStruct((n, B_W), f32),
        ],
        input_output_aliases=aliases,
        compiler_params=_cp(("arbitrary",), VMEM_LIMIT),
        name=f"dil_sample_g{g}",
    )(*args)


def _sb_local(z, valid, umat, bits=False):
    zpos = jnp.maximum(z, 0.0)
    zneg = jnp.minimum(z, 0.0)
    if bits:
        lg = jnp.log(1.0 + jnp.exp2(zneg - zpos)) * LOG2E
    else:
        lg = jnp.log(1.0 + jnp.exp(zneg - zpos))
    sp = zpos + lg
    if valid is not None:
        sp = jnp.where(valid, sp, 0.0)
    later = jnp.dot(sp.astype(bf16), umat, preferred_element_type=z.dtype)
    return zneg - lg - later, jnp.sum(sp.astype(f32), axis=-1, keepdims=True)


def _sb_finish(expo, c, valid, bits=False):
    x = expo - c.astype(expo.dtype)
    w = jnp.exp2(x) if bits else jnp.exp(x)
    if valid is not None:
        w = jnp.where(valid, w, 0.0)
    return w.astype(bf16)


def _sb_weights(z, c, valid, umat):
    expo, sps = _sb_local(z, valid, umat)
    return _sb_finish(expo, c, valid), sps


def _later_key_matrix(tk):
    return (lax.broadcasted_iota(jnp.int32, (tk, tk), 0) > lax.broadcasted_iota(jnp.int32, (tk, tk), 1)).astype(bf16)


def _sb_prompt_body(bias_ref, q_ref, kt_ref, vt_ref, o_ref, kb_ref, vb_ref, *, tq, tk, nblk):
    hp = pl.program_id(1)
    qi = pl.program_id(2)
    frow = lax.broadcasted_iota(jnp.int32, (2 * DH_C, 1), 0)

    def bias_slot(hh):
        return (1 - hh) * DH_C

    @pl.when(qi == 0)
    def _():
        for j in range(nblk):
            kj = kt_ref[:, j * tk:(j + 1) * tk]
            vj = vt_ref[:, j * tk:(j + 1) * tk]
            for hh in range(2):
                ones = jnp.logical_or(frow == bias_slot(hh), frow == bias_slot(hh) + 1)
                kb_ref[hh, j] = jnp.where(ones, 1.0, kj).astype(bf16)
                vb_ref[hh, j] = jnp.where(frow // DH_C == hh, vj, 0.0).astype(bf16)

    lane = lax.broadcasted_iota(jnp.int32, (1, 2 * DH_C), 1)
    q = q_ref[...] * (DH_C ** -0.5 * LOG2E)
    qh = []
    for hh in range(2):
        b = jnp.full((1, 2 * DH_C), bias_ref[2 * hp + hh] * LOG2E, f32)
        b_hi = b.astype(bf16).astype(f32)
        tail = jnp.where(lane == bias_slot(hh), b_hi, jnp.where(lane == bias_slot(hh) + 1, b - b_hi, 0.0))
        qh.append(jnp.where(lane // DH_C == hh, q, tail).astype(bf16))
    umat = _later_key_matrix(tk)
    tri = lax.broadcasted_iota(jnp.int32, (tk, tk), 1) < lax.broadcasted_iota(jnp.int32, (tk, tk), 0)

    def logits(q, hh, kb):
        return jnp.dot(q, kb_ref[hh, kb], preferred_element_type=SB_CHAIN_DTYPE)

    def pair(s, carry):
        cs, acc = carry
        new_cs = []
        for hh in range(2):
            parts = []
            for kb in (2 * s + 1, 2 * s):
                expo, sps = _sb_local(logits(qh[hh], hh, kb), None, umat, bits=True)
                parts.append((kb, expo, sps))
            c = cs[hh]
            for kb, expo, sps in parts:
                acc = acc + _dot_nt(_sb_finish(expo, c, None, bits=True), vb_ref[hh, kb])
                c = c + sps
            new_cs.append(c)
        return tuple(new_cs), acc

    def diagonal(s):
        zero_c = jnp.zeros((tk, 1), f32)
        acc_lo = jnp.zeros((tk, 2 * DH_C), f32)
        acc_hi = jnp.zeros((tk, 2 * DH_C), f32)
        cs = []
        for hh in range(2):
            q_lo, q_hi = qh[hh][:tk], qh[hh][tk:]
            e_h1, s_h1 = _sb_local(logits(q_hi, hh, 2 * s + 1), tri, umat, bits=True)
            e_h0, s_h0 = _sb_local(logits(q_hi, hh, 2 * s), None, umat, bits=True)
            e_l0, s_l0 = _sb_local(logits(q_lo, hh, 2 * s), tri, umat, bits=True)
            acc_hi = (acc_hi + _dot_nt(_sb_finish(e_h1, zero_c, tri, bits=True), vb_ref[hh, 2 * s + 1])
                      + _dot_nt(_sb_finish(e_h0, s_h1, None, bits=True), vb_ref[hh, 2 * s]))
            acc_lo = acc_lo + _dot_nt(_sb_finish(e_l0, zero_c, tri, bits=True), vb_ref[hh, 2 * s])
            cs.append(jnp.concatenate([s_l0, s_h1 + s_h0], axis=0))
        return tuple(cs), jnp.concatenate([acc_lo, acc_hi], axis=0)

    _, acc = lax.fori_loop(0, qi, lambda s, c: pair(qi - 1 - s, c), diagonal(qi))
    o_ref[...] = acc


def _sb_prompt(q_rows, kt, vt, bias, i, nb, t):
    n = q_rows.shape[0]
    tq, tk = SB_TQ, SB_TK
    assert t % tq == 0 and tq == 2 * tk
    nq = t // tq
    nblk = t // tk
    nhp = H_C // 2
    w = 2 * DH_C
    return pl.pallas_call(
        functools.partial(_sb_prompt_body, tq=tq, tk=tk, nblk=nblk),
        grid_spec=pltpu.PrefetchScalarGridSpec(
            num_scalar_prefetch=1,
            grid=(nb, nhp, nq),
            in_specs=[
                pl.BlockSpec((tq, w), lambda b, hp, qi, bias: (b * nq + qi, hp)),
                pl.BlockSpec((None, None, w, t), lambda b, hp, qi, bias: (i, b, hp, 0)),
                pl.BlockSpec((None, None, w, t), lambda b, hp, qi, bias: (i, b, hp, 0)),
            ],
            out_specs=pl.BlockSpec((tq, w), lambda b, hp, qi, bias: (b * nq + qi, hp)),
            scratch_shapes=[
                pltpu.VMEM((2, nblk, w, tk), bf16),
                pltpu.VMEM((2, nblk, w, tk), bf16),
            ],
        ),
        out_shape=jax.ShapeDtypeStruct((n, ODD_W), f32),
        compiler_params=_cp(("arbitrary", "arbitrary", "arbitrary"), VMEM_LIMIT),
        name="sb_prompt",
    )(bias, q_rows, kt, vt)


def _sb_sample_body(*refs, dt, nstep, group):
    q_ref, kvn_ref = refs[1:3]
    kc_refs = refs[3:3 + group]
    vc_refs = refs[3 + group:3 + 2 * group]
    bias_ref = refs[3 + 2 * group]
    o_ref, qbd_ref, acc_ref, c_ref = refs[-4:]
    p = pl.program_id(1)
    nrow = H_C * dt
    umat = _later_key_matrix(PAGE_SIZE)
    bias = bias_ref[...]

    @pl.when(p == 0)
    def _():
        lane = lax.broadcasted_iota(jnp.int32, (1, ODD_W), 1) // DH_C
        q = q_ref[...] * (DH_C ** -0.5)
        qbd = jnp.concatenate([jnp.where(lane == h, q, 0.0) for h in range(H_C)], axis=0).astype(bf16)
        qbd_ref[...] = qbd
        pad = jnp.zeros((PAGE_SIZE - dt, ODD_W), f32)
        kblk = jnp.concatenate([kvn_ref[:, 0:ODD_W], pad], axis=0).astype(bf16)
        vblk = jnp.concatenate([kvn_ref[:, ODD_W:2 * ODD_W], pad], axis=0).astype(bf16)
        qidx = jnp.bitwise_and(lax.broadcasted_iota(jnp.int32, (nrow, PAGE_SIZE), 0), dt - 1)
        valid = lax.broadcasted_iota(jnp.int32, (nrow, PAGE_SIZE), 1) < qidx
        w, sps = _sb_weights(_dot_nt(qbd, kblk) + bias, jnp.zeros((nrow, 1), f32), valid, umat)
        acc_ref[...] = _dot(w, vblk)
        c_ref[...] = sps

    @pl.when(p > 0)
    def _():
        qbd = qbd_ref[...]
        c = c_ref[...]
        ws = []
        for j in range(group):
            kt = kc_refs[j][...].reshape(ODD_W, PAGE_SIZE).astype(bf16)
            w, sps = _sb_weights(_dot(qbd, kt) + bias, c, None, umat)
            ws.append(w)
            c = c + sps
        c_ref[...] = c
        vt = jnp.concatenate([vc_refs[j][...].reshape(ODD_W, PAGE_SIZE).astype(bf16) for j in range(group)], axis=1)
        acc_ref[...] += _dot_nt(jnp.concatenate(ws, axis=1), vt)

    @pl.when(p == nstep)
    def _():
        lane = lax.broadcasted_iota(jnp.int32, (1, ODD_W), 1) // DH_C
        out = jnp.zeros((dt, ODD_W), f32)
        for h in range(H_C):
            out = jnp.where(lane == h, acc_ref[h * dt:(h + 1) * dt, :], out)
        o_ref[...] = out


def _sb_sample(q_rows, kv_new, cache_k, cache_v, page_table, bias_col, o_prev, i, db, dt, row0):
    n = q_rows.shape[0]
    npage = page_table.shape[1]
    group = math.gcd(npage, SB_PAGES_PER_STEP)
    nstep = npage // group
    kc = jnp.transpose(cache_k, (0, 1, 3, 4, 2))
    vc = jnp.transpose(cache_v, (0, 1, 3, 4, 2))
    rb0 = row0 // dt
    nrow = H_C * dt

    def page(j):
        return lambda b, p, pt: (i, pt[b, npage - 1 - (jnp.maximum(p, 1) - 1) * group - j], 0, 0, 0)

    page_specs = [pl.BlockSpec((None, None, H_C, DH_C, PAGE_SIZE), page(j)) for j in range(group)]
    return pl.pallas_call(
        functools.partial(_sb_sample_body, dt=dt, nstep=nstep, group=group),
        grid_spec=pltpu.PrefetchScalarGridSpec(
            num_scalar_prefetch=1,
            grid=(db, nstep + 1),
            in_specs=[
                pl.BlockSpec((dt, ODD_W), lambda b, p, pt: (rb0 + b, 0)),
                pl.BlockSpec((dt, 2 * ODD_W), lambda b, p, pt: (b, 0)),
            ] + page_specs + page_specs + [
                pl.BlockSpec((nrow, 1), lambda b, p, pt: (0, 0)),
                pl.BlockSpec(memory_space=pl.ANY),
            ],
            out_specs=pl.BlockSpec((dt, ODD_W), lambda b, p, pt: (rb0 + b, 0)),
            scratch_shapes=[
                pltpu.VMEM((nrow, ODD_W), bf16),
                pltpu.VMEM((nrow, ODD_W), f32),
                pltpu.VMEM((nrow, 1), f32),
            ],
        ),
        out_shape=jax.ShapeDtypeStruct((n, ODD_W), f32),
        input_output_aliases={3 + 2 * group + 1: 0},
        compiler_params=_cp(("arbitrary", "arbitrary"), VMEM_LIMIT),
        name="sb_sample",
    )(page_table, q_rows, kv_new, *([kc] * group), *([vc] * group), bias_col, o_prev)


def _win_tail_body(k_ref, v_ref, *rest):
    o_ref = rest[-1]
    o_ref[0:B_W, :] = k_ref[...].T
    o_ref[B_W:2 * B_W, :] = v_ref[...].T


def _win_tail(p, stack, g, i, n_even, nb, t, clen):
    assert t >= clen
    tb = min(clen, 2 * LANE)
    assert clen % tb == 0 and (t - clen) % tb == 0
    rows = lambda c0: pl.BlockSpec((tb, B_W), lambda b, j: (b * (t // tb) + (t - clen) // tb + j, c0 // B_W))
    args = [p, p]
    in_specs = [rows(COL_KB + g * B_W), rows(COL_VB + g * B_W)]
    aliases = {}
    if stack is not None:
        args.append(stack)
        in_specs.append(pl.BlockSpec(memory_space=pl.ANY))
        aliases = {2: 0}
    return pl.pallas_call(
        _win_tail_body,
        grid=(nb, clen // tb),
        in_specs=in_specs,
        out_specs=pl.BlockSpec((None, None, 2 * B_W, tb), lambda b, j: (i, b, 0, j)),
        out_shape=jax.ShapeDtypeStruct((n_even, nb, 2 * B_W, clen), f32),
        input_output_aliases=aliases,
        compiler_params=_cp(("arbitrary", "arbitrary")),
        name=f"win_tail_g{g}",
    )(*args)


def _token_major(a):
    n_odd, nb, _, t = a.shape
    return jnp.transpose(a.reshape(n_odd, nb, H_C, DH_C, t), (0, 1, 4, 2, 3))


def _win_rows(c):
    n_even, db, _, clen = c.shape
    return jnp.transpose(c.reshape(n_even, db, 2, H_BG, DH_B, clen), (0, 1, 5, 2, 3, 4))


def kernel(x_prompt, x_sample, state_gla, cache_win_g0, cache_win_g1, cache_win_g2, cache_sb_k, cache_sb_v, page_table, norm_pre, norm_post, ffn_w_gate, ffn_w_up, ffn_w_down, even_w_in, gla_w_gate, gla_b_gate, gla_norm, even_w_out, odd_w_in, odd_w_out, sb_bias):
    nb, t, d = x_prompt.shape
    db, dt, _ = x_sample.shape
    depth = norm_pre.shape[0]
    n_p, n_s = nb * t, db * dt
    win_caches = tuple(jnp.transpose(c, (0, 1, 3, 4, 5, 2)).reshape(c.shape[0], db, 2 * B_W, c.shape[2])
                       for c in (cache_win_g0, cache_win_g1, cache_win_g2))
    assert dt == SUBLANE and n_p % (16 * SUBLANE) == 0 and n_s % 16 == 0

    wg = ffn_w_gate.astype(bf16)
    wu = ffn_w_up.astype(bf16)
    wd = ffn_w_down.astype(bf16)
    g0, g1 = COL_QB, COL_QB + GATE_RANK
    w_main = jnp.concatenate([even_w_in[:, :, :g0], even_w_in[:, :, g1:]], axis=2).astype(bf16)
    w_gate_in = jnp.pad(even_w_in[:, :, g0:g1], ((0, 0), (0, 0), (0, LANE - GATE_RANK))).astype(bf16)
    gla_gw = jnp.pad(gla_w_gate, ((0, 0), (0, LANE - GATE_RANK), (0, 0))).astype(bf16)
    gla_gb = gla_b_gate[:, None, :]
    gla_gn = gla_norm[:, None, :]
    w_out_e = even_w_out.astype(bf16)
    w_in_o = odd_w_in.astype(bf16)
    w_out_o = odd_w_out.astype(bf16)
    npre = norm_pre[:, :, None, :]
    npost = norm_post[:, :, None, :]
    bias_cols = jnp.repeat(sb_bias, dt, axis=1)[:, :, None]

    x = (x_prompt.reshape(n_p, d), x_sample.reshape(n_s, d))
    c_p = GLA_CHUNK if t % GLA_CHUNK == 0 else t
    c_s = GLA_CHUNK if dt % GLA_CHUNK == 0 else dt

    gla_p, gla_s = [], []
    win_p, win_s = [None for _ in DIL_PAIRS], [None for _ in DIL_PAIRS]
    kt_all, vt_all, sbk_s, sbv_s = None, None, [], []
    for l in range(depth):
        i = l // 2
        x = _ffn(x, npre, wg, wu, wd, npost, l, 0, n_p)
        if l % 2 == 0:
            p, la, *p_grp = _proj_even(x, npre, w_main, w_gate_in, gla_gw, gla_gb, l, i)
            oa_p, sp_new = _gla(p, la, jnp.zeros((nb, H_A, DK_A, DV_A), f32), gla_gn, i, nb, t // c_p, c_p, 0,
                                math.gcd(nb, GLA_SEQS_PER_STEP))
            oa_s, ss_new = _gla(p, la, state_gla[i], gla_gn, i, db, dt // c_s, c_s, n_p,
                                math.gcd(db, 2 * GLA_SEQS_PER_STEP))
            gla_p.append(sp_new)
            gla_s.append(ss_new)
            og, lg = [], []
            for g, (window, _) in enumerate(DIL_PAIRS):
                if g == 0:
                    o_g, l_g = _dil_prompt(p, (COL_QB, COL_KB, COL_VB), g, nb, t)
                else:
                    o_g, l_g = _dil_prompt(p_grp[g - 1], (0, B_W, 2 * B_W), g, nb, t)
                win_s[g], o_g, l_g = _dil_sample(p, win_caches[g], win_s[g], o_g, l_g, g, i, db, dt, n_p)
                og.append(o_g)
                lg.append(l_g)
                win_p[g] = _win_tail(p, win_p[g], g, i, win_caches[g].shape[0], nb, t, win_caches[g].shape[3])
            x = _post_even(x, (oa_p, oa_s), og, lg, w_out_e, npost, l, i)
        else:
            q_rows, kt_all, vt_all, kv_new = _proj_odd(x, npre, w_in_o, l, i, nb, t, kt_all, vt_all)
            o = _sb_prompt(q_rows, kt_all, vt_all, sb_bias[i], i, nb, t)
            o = _sb_sample(q_rows, kv_new, cache_sb_k, cache_sb_v, page_table, bias_cols[i], o, i, db, dt, n_p)
            sbk_s.append(kv_new[:, :ODD_W].reshape(db, dt, H_C, DH_C))
            sbv_s.append(kv_new[:, ODD_W:].reshape(db, dt, H_C, DH_C))
            x = _post_odd(x, o, w_out_o, npost, l, i)
        x = _ffn(x, npre, wg, wu, wd, npost, l, 1, n_p, split_out=(l == depth - 1))

    return (x[0].reshape(nb, t, d), x[1].reshape(db, dt, d),
            jnp.stack(gla_p), jnp.stack(gla_s),
            _win_rows(win_p[0]), _win_rows(win_p[1]), _win_rows(win_p[2]),
            _win_rows(win_s[0]), _win_rows(win_s[1]), _win_rows(win_s[2]),
            _token_major(kt_all), _token_major(vt_all), jnp.stack(sbk_s), jnp.stack(sbv_s))
```

```python
import functools
import math

import jax
import jax.numpy as jnp
from jax import lax
from jax.experimental import pallas as pl
from jax.experimental.pallas import tpu as pltpu

f32 = jnp.float32
bf16 = jnp.bfloat16

EPS = 1e-6
H_A, DK_A, DV_A = 4, 64, 128
GATE_RANK = 16
GATE_NORM = 16.0
GLA_CHUNK = 64
DIL_PAIRS = ((128, 1), (512, 4), (2048, 16))
N_DIL = 3
H_BG, DH_B = 4, 64
H_C, DH_C = 16, 64
PAGE_SIZE = 128

A_Q, A_K, A_V, A_R = H_A * DK_A, H_A * DK_A, H_A * DV_A, H_A * DV_A
B_W = H_BG * DH_B
EVEN_MAIN = A_Q + A_K + A_V + A_R + 3 * N_DIL * B_W
COL_QB = A_Q + A_K + A_V + A_R
COL_KB = COL_QB + N_DIL * B_W
COL_VB = COL_KB + N_DIL * B_W
ODD_W = H_C * DH_C

LANE = 128
SUBLANE = 8
NEG = -1e30
LOG2E = 1.4426950408889634
VMEM_LIMIT = 56 * 1024 * 1024
GLA_SEQS_PER_STEP = 4
SB_CHAIN_DTYPE = f32
SB_PAGES_PER_STEP = 16
SB_TQ, SB_TK = 512, 256


def _cp(sem, vmem=None):
    return pltpu.CompilerParams(dimension_semantics=sem, vmem_limit_bytes=vmem)


def _rms(x, g):
    return x * lax.rsqrt(jnp.mean(x * x, axis=-1, keepdims=True) + EPS) * g


def _dot(a, b):
    return jnp.dot(a, b, preferred_element_type=f32)


def _dot_nt(a, b):
    return lax.dot_general(a, b, (((1,), (1,)), ((), ())), preferred_element_type=f32)


def _dot_tn(a, b):
    return lax.dot_general(a, b, (((0,), (0,)), ((), ())), preferred_element_type=f32)


def _row_tile(n):
    for t in (256, 128, 64, 32, 16, 8):
        if n % t == 0:
            return t
    raise ValueError(f"row count {n} is not a multiple of {SUBLANE}")


def _mix_even(refs, is_prompt):
    oap_ref, oas_ref, o0_ref, o1_ref, o2_ref, l0_ref, l1_ref, l2_ref, w_ref, g_ref = refs
    oa = jnp.where(is_prompt, oap_ref[...], oas_ref[...])
    l0, l1, l2 = l0_ref[...], l1_ref[...], l2_ref[...]
    mx = jnp.maximum(jnp.maximum(l0, l1), l2)
    e0, e1, e2 = jnp.exp(l0 - mx), jnp.exp(l1 - mx), jnp.exp(l2 - mx)
    den = e0 + e1 + e2
    ob = (e0 / den) * o0_ref[...] + (e1 / den) * o1_ref[...] + (e2 / den) * o2_ref[...]
    m = _dot(oa.astype(bf16), w_ref[0:A_V, :]) + _dot(ob.astype(bf16), w_ref[A_V:A_V + B_W, :])
    return _rms(m, g_ref[...])


def _mix_odd(refs):
    o_ref, w_ref, g_ref = refs
    return _rms(_dot(o_ref[...].astype(bf16), w_ref[...]), g_ref[...])


def _ffn_body(*refs, n_ptiles, split_in, split_out, mix, n_mix):
    r = pl.program_id(0)
    n_in = (2 if split_in else 1) + n_mix
    gpre_ref, wg_ref, wu_ref, wd_ref, gpost_ref = refs[n_in:n_in + 5]
    outs = refs[n_in + 5:]
    x = jnp.where(r < n_ptiles, refs[0][...], refs[1][...]) if split_in else refs[0][...]
    if mix == "even":
        x = x + _mix_even(refs[n_in - n_mix:n_in], r < n_ptiles)
    elif mix == "odd":
        x = x + _mix_odd(refs[n_in - n_mix:n_in])
    h = _rms(x, gpre_ref[...]).astype(bf16)
    gate = _dot(h, wg_ref[...])
    up = _dot(h, wu_ref[...])
    a = (gate * jax.nn.sigmoid(gate) * up).astype(bf16)
    y = x + 0.5 * _rms(_dot(a, wd_ref[...]), gpost_ref[...])
    if split_out:
        @pl.when(r < n_ptiles)
        def _():
            outs[0][...] = y

        @pl.when(r >= n_ptiles)
        def _():
            outs[1][...] = y
    else:
        outs[0][...] = y


def _ffn(xs, gpre, wg, wu, wd, gpost, l, j, n_p, split_out=False, mix=None, mix_args=(), mix_w=None, i=0):
    split_in = isinstance(xs, tuple)
    xs = xs if split_in else (xs,)
    n = sum(a.shape[0] for a in xs)
    d = xs[0].shape[1]
    ff = wg.shape[-1]
    tm = _row_tile(n)
    assert n_p % tm == 0
    n_ptiles = n_p // tm
    once = pl.Buffered(1)
    prompt_rows = pl.BlockSpec((tm, d), lambda r: (jnp.minimum(r, n_ptiles - 1), 0))
    sample_rows = pl.BlockSpec((tm, d), lambda r: (jnp.maximum(r - n_ptiles, 0), 0))
    all_rows = pl.BlockSpec((tm, d), lambda r: (r, 0))
    if split_out:
        out_specs = [prompt_rows, sample_rows]
        out_shape = [jax.ShapeDtypeStruct((n_p, d), f32), jax.ShapeDtypeStruct((n - n_p, d), f32)]
    else:
        out_specs, out_shape = all_rows, jax.ShapeDtypeStruct((n, d), f32)
    rows_of = lambda w: pl.BlockSpec((tm, w), lambda r: (r, 0))
    mix_specs, mix_ops = [], []
    if mix == "even":
        (oa_p, oa_s), og, lg = mix_args
        assert oa_p.shape[0] == n_p and oa_s.shape[0] % tm == 0
        mix_specs = [pl.BlockSpec((tm, A_V), lambda r: (jnp.minimum(r, n_ptiles - 1), 0)),
                     pl.BlockSpec((tm, A_V), lambda r: (jnp.maximum(r - n_ptiles, 0), 0))] + [rows_of(B_W)] * 6
        mix_ops = [oa_p, oa_s, *og, *lg]
    elif mix == "odd":
        mix_specs = [rows_of(ODD_W)]
        mix_ops = list(mix_args)
    if mix is not None:
        mix_specs += [pl.BlockSpec((None,) + mix_w.shape[1:], lambda r: (i, 0, 0), pipeline_mode=once),
                      pl.BlockSpec((None, None, 1, d), lambda r: (l, 1, 0, 0))]
        mix_ops += [mix_w, gpost]
    return pl.pallas_call(
        functools.partial(_ffn_body, n_ptiles=n_ptiles, split_in=split_in, split_out=split_out, mix=mix,
                          n_mix=len(mix_ops)),
        grid=(n // tm,),
        in_specs=([prompt_rows, sample_rows] if split_in else [all_rows]) + mix_specs + [
            pl.BlockSpec((None, None, 1, d), lambda r: (l, 2 * j, 0, 0)),
            pl.BlockSpec((None, None, d, ff), lambda r: (l, j, 0, 0), pipeline_mode=once),
            pl.BlockSpec((None, None, d, ff), lambda r: (l, j, 0, 0), pipeline_mode=once),
            pl.BlockSpec((None, None, ff, d), lambda r: (l, j, 0, 0), pipeline_mode=once),
            pl.BlockSpec((None, None, 1, d), lambda r: (l, 2 * j, 0, 0)),
        ],
        out_specs=out_specs,
        out_shape=out_shape,
        compiler_params=_cp(("arbitrary",), VMEM_LIMIT),
        name="ffn",
    )(*xs, *mix_ops, gpre, wg, wu, wd, gpost)


def _proj_even_body(x_ref, g_ref, wm_ref, wgate_ref, gw_ref, gb_ref, p_ref, la_ref, *grp_refs):
    h = _rms(x_ref[...], g_ref[...]).astype(bf16)
    p = _dot(h, wm_ref[...])
    p_ref[...] = p
    for g, ref in enumerate(grp_refs, start=1):
        ref[...] = jnp.concatenate([p[:, c + g * B_W:c + (g + 1) * B_W] for c in (COL_QB, COL_KB, COL_VB)],
                                   axis=1).astype(bf16)
    ga = _dot(h, wgate_ref[...])
    lg = _dot(ga.astype(bf16), gw_ref[...]) + gb_ref[...]
    la_ref[...] = (jnp.minimum(lg, 0.0) - jnp.log1p(jnp.exp(-jnp.abs(lg)))) / GATE_NORM


def _proj_even(x, norm_pre, wm, wgate, gw, gb, l, i):
    n, d = x.shape
    tm = _row_tile(n)
    once = pl.Buffered(1)
    return pl.pallas_call(
        _proj_even_body,
        grid=(n // tm,),
        in_specs=[
            pl.BlockSpec((tm, d), lambda r: (r, 0)),
            pl.BlockSpec((None, None, 1, d), lambda r: (l, 1, 0, 0)),
            pl.BlockSpec((None, d, EVEN_MAIN), lambda r: (i, 0, 0), pipeline_mode=once),
            pl.BlockSpec((None, d, LANE), lambda r: (i, 0, 0), pipeline_mode=once),
            pl.BlockSpec((None, LANE, A_K), lambda r: (i, 0, 0), pipeline_mode=once),
            pl.BlockSpec((None, 1, A_K), lambda r: (i, 0, 0)),
        ],
        out_specs=[
            pl.BlockSpec((tm, EVEN_MAIN), lambda r: (r, 0)),
            pl.BlockSpec((tm, A_K), lambda r: (r, 0)),
        ] + [pl.BlockSpec((tm, 3 * B_W), lambda r: (r, 0))] * (N_DIL - 1),
        out_shape=[
            jax.ShapeDtypeStruct((n, EVEN_MAIN), f32),
            jax.ShapeDtypeStruct((n, A_K), f32),
        ] + [jax.ShapeDtypeStruct((n, 3 * B_W), bf16)] * (N_DIL - 1),
        compiler_params=_cp(("arbitrary",), VMEM_LIMIT),
        name="proj_even",
    )(x, norm_pre, wm, wgate, gw, gb)


def _proj_odd_body(*refs, n_ptiles):
    x_ref, g_ref, w_ref = refs[:3]
    q_ref, kt_ref, vt_ref, kvs_ref = refs[-4:]
    r = pl.program_id(0)
    h = _rms(x_ref[...], g_ref[...]).astype(bf16)
    p = _dot(h, w_ref[...])
    q_ref[...] = p[:, :ODD_W]

    @pl.when(r < n_ptiles)
    def _():
        kt_ref[...] = p[:, ODD_W:2 * ODD_W].T
        vt_ref[...] = p[:, 2 * ODD_W:].T

    @pl.when(r >= n_ptiles)
    def _():
        kvs_ref[...] = p[:, ODD_W:]


def _proj_odd(x, norm_pre, w, l, i, nb, t, kt_prev, vt_prev):
    n, d = x.shape
    tm = _row_tile(n)
    assert t % tm == 0
    n_odd = w.shape[0]
    tpb = t // tm
    n_ptiles = nb * tpb
    n_s = n - nb * t

    def kt_map(r):
        rp = jnp.minimum(r, n_ptiles - 1)
        return (i, rp // tpb, 0, rp % tpb)

    args = [x, norm_pre, w]
    in_specs = [
        pl.BlockSpec((tm, d), lambda r: (r, 0)),
        pl.BlockSpec((None, None, 1, d), lambda r: (l, 1, 0, 0)),
        pl.BlockSpec((None, d, 3 * ODD_W), lambda r: (i, 0, 0), pipeline_mode=pl.Buffered(1)),
    ]
    aliases = {}
    if kt_prev is not None:
        args += [kt_prev, vt_prev]
        in_specs += [pl.BlockSpec(memory_space=pl.ANY)] * 2
        aliases = {3: 1, 4: 2}
    stack = jax.ShapeDtypeStruct((n_odd, nb, ODD_W, t), f32)
    return pl.pallas_call(
        functools.partial(_proj_odd_body, n_ptiles=n_ptiles),
        grid=(n // tm,),
        in_specs=in_specs,
        out_specs=[
            pl.BlockSpec((tm, ODD_W), lambda r: (r, 0)),
            pl.BlockSpec((None, None, ODD_W, tm), kt_map),
            pl.BlockSpec((None, None, ODD_W, tm), kt_map),
            pl.BlockSpec((tm, 2 * ODD_W), lambda r: (jnp.maximum(r - n_ptiles, 0), 0)),
        ],
        out_shape=[jax.ShapeDtypeStruct((n, ODD_W), f32), stack, stack,
                   jax.ShapeDtypeStruct((n_s, 2 * ODD_W), f32)],
        input_output_aliases=aliases,
        compiler_params=_cp(("arbitrary",), VMEM_LIMIT),
        name="proj_odd",
    )(*args)


def _gla_body(*refs, c, bpg, nchunk):
    s0_ref, gn_ref, o_ref, s_ref, sbd_ref = refs[5 * bpg:]
    n = pl.program_id(1)
    dk_all, dv_all = H_A * DK_A, H_A * DV_A
    on_diag = (lax.broadcasted_iota(jnp.int32, (dk_all, dv_all), 0) // DK_A
               == lax.broadcasted_iota(jnp.int32, (dk_all, dv_all), 1) // DV_A)

    @pl.when(n == 0)
    def _():
        for j in range(bpg):
            rows = [jnp.concatenate([s0_ref[j, h] if g == h else jnp.zeros((DK_A, DV_A), f32) for g in range(H_A)],
                                    axis=1) for h in range(H_A)]
            sbd_ref[j] = jnp.concatenate(rows, axis=0)

    causal = (lax.broadcasted_iota(jnp.int32, (c, c), 1) <= lax.broadcasted_iota(jnp.int32, (c, c), 0))
    ltri = causal.astype(bf16)
    causal_h = (lax.broadcasted_iota(jnp.int32, (H_A * c, c), 1)
                <= jnp.bitwise_and(lax.broadcasted_iota(jnp.int32, (H_A * c, c), 0), c - 1))
    eye = (lax.broadcasted_iota(jnp.int32, (dk_all, dk_all), 0) == lax.broadcasted_iota(jnp.int32, (dk_all, dk_all), 1))
    khead = lax.broadcasted_iota(jnp.int32, (1, dk_all), 1) // DK_A
    vhead = lax.broadcasted_iota(jnp.int32, (1, dv_all), 1) // DV_A
    gn = gn_ref[...]
    for j in range(bpg):
        q_ref, k_ref, v_ref, r_ref, la_ref = refs[5 * j:5 * j + 5]
        q = q_ref[...] * (DK_A ** -0.5)
        k = k_ref[...]
        v = v_ref[...].astype(bf16)
        r = r_ref[...]
        s_old = sbd_ref[j]
        la = la_ref[...]
        la_hi = la.astype(bf16)
        la_lo = (la - la_hi.astype(f32)).astype(bf16)
        cum = _dot(ltri, la_hi) + _dot(ltri, la_lo)
        mid = cum[c // 2:c // 2 + 1, :]
        last = cum[c - 1:c, :]
        q_in = q * jnp.exp(cum - mid)
        k_in = (k * jnp.exp(mid - cum)).astype(bf16)
        q_st = (q * jnp.exp(cum)).astype(bf16)
        k_st = (k * jnp.exp(last - cum)).astype(bf16)
        q_heads = jnp.concatenate([jnp.where(khead == h, q_in, 0.0) for h in range(H_A)], axis=0).astype(bf16)
        scores = jnp.where(causal_h, _dot_nt(q_heads, k_in), 0.0)
        o_all = _dot(scores.astype(bf16), v)
        o = _dot(q_st, s_old.astype(bf16))
        for h in range(H_A):
            o = o + jnp.where(vhead == h, o_all[h * c:(h + 1) * c, :], 0.0)
        e_col = jnp.sum(jnp.where(eye, jnp.broadcast_to(jnp.exp(last), (dk_all, dk_all)), 0.0), axis=1, keepdims=True)
        sbd_ref[j] = e_col * s_old + jnp.where(on_diag, _dot_tn(k_st, v), 0.0)
        for h in range(H_A):
            sv = slice(h * DV_A, (h + 1) * DV_A)
            rh = r[:, sv]
            o_ref[j, :, sv] = _rms(o[:, sv], gn) * (rh * jax.nn.sigmoid(rh))

    @pl.when(n == nchunk - 1)
    def _():
        for j in range(bpg):
            for h in range(H_A):
                s_ref[j, h] = sbd_ref[j, h * DK_A:(h + 1) * DK_A, h * DV_A:(h + 1) * DV_A]


def _gla(p, la, s0, gn, i, nb, nchunk, c, row0, bpg):
    assert nb % bpg == 0
    rb0 = row0 // c
    in_specs = []
    args = []
    for j in range(bpg):
        rowmap = lambda cb, j=j: (lambda g, t: (rb0 + (g * bpg + j) * nchunk + t, cb))
        in_specs += [
            pl.BlockSpec((c, A_Q), rowmap(0)),
            pl.BlockSpec((c, A_K), rowmap(1)),
            pl.BlockSpec((c, A_V), rowmap(1)),
            pl.BlockSpec((c, A_R), rowmap(2)),
            pl.BlockSpec((c, A_K), rowmap(0)),
        ]
        args += [p, p, p, p, la]
    in_specs += [
        pl.BlockSpec((bpg, H_A, DK_A, DV_A), lambda g, t: (g, 0, 0, 0)),
        pl.BlockSpec((None, 1, DV_A), lambda g, t: (i, 0, 0)),
    ]
    args += [s0, gn]
    o, s_new = pl.pallas_call(
        functools.partial(_gla_body, c=c, bpg=bpg, nchunk=nchunk),
        grid=(nb // bpg, nchunk),
        in_specs=in_specs,
        out_specs=[
            pl.BlockSpec((bpg, c, A_V), lambda g, t: (g, t, 0)),
            pl.BlockSpec((bpg, H_A, DK_A, DV_A), lambda g, t: (g, 0, 0, 0)),
        ],
        out_shape=[
            jax.ShapeDtypeStruct((nb, nchunk * c, A_V), f32),
            jax.ShapeDtypeStruct((nb, H_A, DK_A, DV_A), f32),
        ],
        scratch_shapes=[pltpu.VMEM((bpg, H_A * DK_A, H_A * DV_A), f32)],
        compiler_params=_cp(("arbitrary", "arbitrary"), VMEM_LIMIT),
        name="gla",
    )(*args)
    return o.reshape(nb * nchunk * c, A_V), s_new


def _head_lane(width):
    return lax.broadcasted_iota(jnp.int32, (1, width), 1) // DH_B


def _dil_prompt_body(q_ref, k_ref, v_ref, o_ref, lse_ref, *, nblk, qb):
    row = jnp.bitwise_and(lax.broadcasted_iota(jnp.int32, (H_BG * qb, 2 * qb), 0), qb - 1)
    col = lax.broadcasted_iota(jnp.int32, (H_BG * qb, 2 * qb), 1)
    band = jnp.logical_and(col >= row, col <= row + qb)
    lane = _head_lane(B_W)

    def blk(i, carry):
        q0 = pl.multiple_of(i * qb, qb)
        p0 = pl.multiple_of(jnp.maximum(i - 1, 0) * qb, qb)
        q = q_ref[pl.ds(q0, qb), :] * (DH_B ** -0.5)
        qs = jnp.concatenate([jnp.where(lane == h, q, 0.0) for h in range(H_BG)], axis=0).astype(bf16)
        kk = jnp.concatenate([k_ref[pl.ds(p0, qb), :], k_ref[pl.ds(q0, qb), :]], axis=0).astype(bf16)
        vv = jnp.concatenate([v_ref[pl.ds(p0, qb), :], v_ref[pl.ds(q0, qb), :]], axis=0).astype(bf16)
        valid = jnp.logical_and(band, jnp.logical_or(col >= qb, i > 0))
        s = jnp.where(valid, _dot_nt(qs, kk), NEG)
        m = jnp.max(s, axis=-1, keepdims=True)
        pr = jnp.exp(s - m)
        den = jnp.sum(pr, axis=-1, keepdims=True)
        o = _dot(pr.astype(bf16), vv) / den
        lse = m + jnp.log(den)
        o_acc = jnp.zeros((qb, B_W), f32)
        l_acc = jnp.zeros((qb, B_W), f32)
        for h in range(H_BG):
            sel = lane == h
            o_acc = jnp.where(sel, o[h * qb:(h + 1) * qb, :], o_acc)
            l_acc = jnp.where(sel, lse[h * qb:(h + 1) * qb, :], l_acc)
        o_ref[pl.ds(q0, qb), :] = o_acc
        lse_ref[pl.ds(q0, qb), :] = l_acc
        return carry

    lax.fori_loop(0, nblk, blk, 0)


def _dil_prompt(p, cols, g, nb, t):
    window, dil = DIL_PAIRS[g]
    assert window // dil == LANE and t % (dil * LANE) == 0
    n, width = p.shape
    ts = t // dil
    qb = LANE
    ncol = width // B_W
    pv = p.reshape(n // dil, dil * width)
    spec = lambda c: pl.BlockSpec((ts, B_W), lambda b, r: (b, r * ncol + c // B_W))
    o, lse = pl.pallas_call(
        functools.partial(_dil_prompt_body, nblk=ts // qb, qb=qb),
        grid=(nb, dil),
        in_specs=[spec(cols[0]), spec(cols[1]), spec(cols[2])],
        out_specs=[pl.BlockSpec((ts, B_W), lambda b, r: (b, r))] * 2,
        out_shape=[jax.ShapeDtypeStruct((n // dil, dil * B_W), f32)] * 2,
        compiler_params=_cp(("arbitrary", "arbitrary"), VMEM_LIMIT),
        name=f"dil_prompt_g{g}",
    )(pv, pv, pv)
    return o.reshape(n, B_W), lse.reshape(n, B_W)


def _dil_sample_body(*refs, clen, window, dil, dt):
    c_ref, q_ref, kn_ref, vn_ref = refs[:4]
    oc_ref, o_ref, lse_ref = refs[-3:]
    kn = kn_ref[...]
    vn = vn_ref[...]
    oc_ref[:, 0:clen - dt] = c_ref[:, dt:clen]
    oc_ref[:, clen - dt:clen] = jnp.concatenate([kn, vn], axis=1).T

    lane = _head_lane(B_W)
    q = q_ref[...] * (DH_B ** -0.5)
    qs = jnp.concatenate([jnp.where(lane == h, q, 0.0) for h in range(H_BG)], axis=0).astype(bf16)
    nq = H_BG * dt
    kc = c_ref[0:B_W, :].astype(bf16)
    vc = c_ref[B_W:2 * B_W, :].astype(bf16)
    qi_c = jnp.bitwise_and(lax.broadcasted_iota(jnp.int32, (nq, clen), 0), dt - 1)
    dist_c = clen + qi_c - lax.broadcasted_iota(jnp.int32, (nq, clen), 1)
    ok_c = jnp.logical_and(jnp.bitwise_and(dist_c, dil - 1) == 0, dist_c <= window)
    qi_n = jnp.bitwise_and(lax.broadcasted_iota(jnp.int32, (nq, dt), 0), dt - 1)
    dist_n = qi_n - lax.broadcasted_iota(jnp.int32, (nq, dt), 1)
    ok_n = jnp.logical_and(dist_n >= 0, jnp.bitwise_and(dist_n, dil - 1) == 0)
    sc = jnp.where(ok_c, _dot(qs, kc), NEG)
    sn = jnp.where(ok_n, _dot_nt(qs, kn.astype(bf16)), NEG)
    m = jnp.maximum(jnp.max(sc, axis=-1, keepdims=True), jnp.max(sn, axis=-1, keepdims=True))
    pc = jnp.exp(sc - m)
    pn = jnp.exp(sn - m)
    den = jnp.sum(pc, axis=-1, keepdims=True) + jnp.sum(pn, axis=-1, keepdims=True)
    num = _dot_nt(pc.astype(bf16), vc) + _dot(pn.astype(bf16), vn.astype(bf16))
    o = num / den
    lse = m + jnp.log(den)
    o_out = jnp.zeros((dt, B_W), f32)
    l_out = jnp.zeros((dt, B_W), f32)
    for h in range(H_BG):
        sel = lane == h
        o_out = jnp.where(sel, o[h * dt:(h + 1) * dt, :], o_out)
        l_out = jnp.where(sel, jnp.broadcast_to(lse[h * dt:(h + 1) * dt, :], (dt, B_W)), l_out)
    o_ref[...] = o_out
    lse_ref[...] = l_out


def _dil_sample(p, cache_t, cache_out, o_prev, l_prev, g, i, db, dt, row0):
    window, dil = DIL_PAIRS[g]
    n_even, _, feat, clen = cache_t.shape
    n = p.shape[0]
    rb0 = row0 // dt
    rows = lambda cb: pl.BlockSpec((dt, B_W), lambda b: (rb0 + b, cb))
    any_spec = pl.BlockSpec(memory_space=pl.ANY)
    args = [cache_t, p, p, p, o_prev, l_prev]
    in_specs = [
        pl.BlockSpec((None, None, feat, clen), lambda b: (i, b, 0, 0)),
        rows(COL_QB // B_W + g), rows(COL_KB // B_W + g), rows(COL_VB // B_W + g),
        any_spec, any_spec,
    ]
    aliases = {4: 1, 5: 2}
    if cache_out is not None:
        args.append(cache_out)
        in_specs.append(any_spec)
        aliases[6] = 0
    return pl.pallas_call(
        functools.partial(_dil_sample_body, clen=clen, window=window, dil=dil, dt=dt),
        grid=(db,),
        in_specs=in_specs,
        out_specs=[
            pl.BlockSpec((None, None, feat, clen), lambda b: (i, b, 0, 0)),
            pl.BlockSpec((dt, B_W), lambda b: (rb0 + b, 0)),
            pl.BlockSpec((dt, B_W), lambda b: (rb0 + b, 0)),
        ],
        out_shape=[
            jax.ShapeDtypeStruct((n_even, db, feat, clen), f32),
            jax.ShapeDtypeStruct((n, B_W), f32),
            jax.ShapeDtypeStruct((n, B_W), f32),
        ],
        input_output_aliases=aliases,
        compiler_params=_cp(("arbitrary",), VMEM_LIMIT),
        name=f"dil_sample_g{g}",
    )(*args)


def _sb_local(z, valid, umat, bits=False):
    zpos = jnp.maximum(z, 0.0)
    zneg = jnp.minimum(z, 0.0)
    if bits:
        lg = jnp.log(1.0 + jnp.exp2(zneg - zpos)) * LOG2E
    else:
        lg = jnp.log(1.0 + jnp.exp(zneg - zpos))
    sp = zpos + lg
    if valid is not None:
        sp = jnp.where(valid, sp, 0.0)
    later = jnp.dot(sp.astype(bf16), umat, preferred_element_type=z.dtype)
    return zneg - lg - later, jnp.sum(sp.astype(f32), axis=-1, keepdims=True)


def _sb_finish(expo, c, valid, bits=False):
    x = expo - c.astype(expo.dtype)
    w = jnp.exp2(x) if bits else jnp.exp(x)
    if valid is not None:
        w = jnp.where(valid, w, 0.0)
    return w.astype(bf16)


def _sb_weights(z, c, valid, umat):
    expo, sps = _sb_local(z, valid, umat)
    return _sb_finish(expo, c, valid), sps


def _later_key_matrix(tk):
    return (lax.broadcasted_iota(jnp.int32, (tk, tk), 0) > lax.broadcasted_iota(jnp.int32, (tk, tk), 1)).astype(bf16)


def _sb_prompt_body(bias_ref, q_ref, kt_ref, vt_ref, o_ref, kb_ref, vb_ref, *, tq, tk, nblk):
    hp = pl.program_id(1)
    qi = pl.program_id(2)
    frow = lax.broadcasted_iota(jnp.int32, (2 * DH_C, 1), 0)

    def bias_slot(hh):
        return (1 - hh) * DH_C

    @pl.when(qi == 0)
    def _():
        for j in range(nblk):
            kj = kt_ref[:, j * tk:(j + 1) * tk]
            vj = vt_ref[:, j * tk:(j + 1) * tk]
            for hh in range(2):
                ones = jnp.logical_or(frow == bias_slot(hh), frow == bias_slot(hh) + 1)
                kb_ref[hh, j] = jnp.where(ones, 1.0, kj).astype(bf16)
                vb_ref[hh, j] = jnp.where(frow // DH_C == hh, vj, 0.0).astype(bf16)

    lane = lax.broadcasted_iota(jnp.int32, (1, 2 * DH_C), 1)
    q = q_ref[...] * (DH_C ** -0.5 * LOG2E)
    qh = []
    for hh in range(2):
        b = jnp.full((1, 2 * DH_C), bias_ref[2 * hp + hh] * LOG2E, f32)
        b_hi = b.astype(bf16).astype(f32)
        tail = jnp.where(lane == bias_slot(hh), b_hi, jnp.where(lane == bias_slot(hh) + 1, b - b_hi, 0.0))
        qh.append(jnp.where(lane // DH_C == hh, q, tail).astype(bf16))
    umat = _later_key_matrix(tk)
    tri = lax.broadcasted_iota(jnp.int32, (tk, tk), 1) < lax.broadcasted_iota(jnp.int32, (tk, tk), 0)

    def logits(q, hh, kb):
        return jnp.dot(q, kb_ref[hh, kb], preferred_element_type=SB_CHAIN_DTYPE)

    def pair(s, carry):
        cs, acc = carry
        new_cs = []
        for hh in range(2):
            parts = []
            for kb in (2 * s + 1, 2 * s):
                expo, sps = _sb_local(logits(qh[hh], hh, kb), None, umat, bits=True)
                parts.append((kb, expo, sps))
            c = cs[hh]
            for kb, expo, sps in parts:
                acc = acc + _dot_nt(_sb_finish(expo, c, None, bits=True), vb_ref[hh, kb])
                c = c + sps
            new_cs.append(c)
        return tuple(new_cs), acc

    def diagonal(s):
        zero_c = jnp.zeros((tk, 1), f32)
        acc_lo = jnp.zeros((tk, 2 * DH_C), f32)
        acc_hi = jnp.zeros((tk, 2 * DH_C), f32)
        cs = []
        for hh in range(2):
            q_lo, q_hi = qh[hh][:tk], qh[hh][tk:]
            e_h1, s_h1 = _sb_local(logits(q_hi, hh, 2 * s + 1), tri, umat, bits=True)
            e_h0, s_h0 = _sb_local(logits(q_hi, hh, 2 * s), None, umat, bits=True)
            e_l0, s_l0 = _sb_local(logits(q_lo, hh, 2 * s), tri, umat, bits=True)
            acc_hi = (acc_hi + _dot_nt(_sb_finish(e_h1, zero_c, tri, bits=True), vb_ref[hh, 2 * s + 1])
                      + _dot_nt(_sb_finish(e_h0, s_h1, None, bits=True), vb_ref[hh, 2 * s]))
            acc_lo = acc_lo + _dot_nt(_sb_finish(e_l0, zero_c, tri, bits=True), vb_ref[hh, 2 * s])
            cs.append(jnp.concatenate([s_l0, s_h1 + s_h0], axis=0))
        return tuple(cs), jnp.concatenate([acc_lo, acc_hi], axis=0)

    _, acc = lax.fori_loop(0, qi, lambda s, c: pair(qi - 1 - s, c), diagonal(qi))
    o_ref[...] = acc


def _sb_prompt(q_rows, kt, vt, bias, i, nb, t):
    n = q_rows.shape[0]
    tq, tk = SB_TQ, SB_TK
    assert t % tq == 0 and tq == 2 * tk
    nq = t // tq
    nblk = t // tk
    nhp = H_C // 2
    w = 2 * DH_C
    return pl.pallas_call(
        functools.partial(_sb_prompt_body, tq=tq, tk=tk, nblk=nblk),
        grid_spec=pltpu.PrefetchScalarGridSpec(
            num_scalar_prefetch=1,
            grid=(nb, nhp, nq),
            in_specs=[
                pl.BlockSpec((tq, w), lambda b, hp, qi, bias: (b * nq + qi, hp)),
                pl.BlockSpec((None, None, w, t), lambda b, hp, qi, bias: (i, b, hp, 0)),
                pl.BlockSpec((None, None, w, t), lambda b, hp, qi, bias: (i, b, hp, 0)),
            ],
            out_specs=pl.BlockSpec((tq, w), lambda b, hp, qi, bias: (b * nq + qi, hp)),
            scratch_shapes=[
                pltpu.VMEM((2, nblk, w, tk), bf16),
                pltpu.VMEM((2, nblk, w, tk), bf16),
            ],
        ),
        out_shape=jax.ShapeDtypeStruct((n, ODD_W), f32),
        compiler_params=_cp(("arbitrary", "arbitrary", "arbitrary"), VMEM_LIMIT),
        name="sb_prompt",
    )(bias, q_rows, kt, vt)


def _sb_sample_body(*refs, dt, nstep, group):
    q_ref, kvn_ref = refs[1:3]
    kc_refs = refs[3:3 + group]
    vc_refs = refs[3 + group:3 + 2 * group]
    bias_ref = refs[3 + 2 * group]
    o_ref, qbd_ref, acc_ref, c_ref = refs[-4:]
    p = pl.program_id(1)
    nrow = H_C * dt
    umat = _later_key_matrix(PAGE_SIZE)
    bias = bias_ref[...]

    @pl.when(p == 0)
    def _():
        lane = lax.broadcasted_iota(jnp.int32, (1, ODD_W), 1) // DH_C
        q = q_ref[...] * (DH_C ** -0.5)
        qbd = jnp.concatenate([jnp.where(lane == h, q, 0.0) for h in range(H_C)], axis=0).astype(bf16)
        qbd_ref[...] = qbd
        pad = jnp.zeros((PAGE_SIZE - dt, ODD_W), f32)
        kblk = jnp.concatenate([kvn_ref[:, 0:ODD_W], pad], axis=0).astype(bf16)
        vblk = jnp.concatenate([kvn_ref[:, ODD_W:2 * ODD_W], pad], axis=0).astype(bf16)
        qidx = jnp.bitwise_and(lax.broadcasted_iota(jnp.int32, (nrow, PAGE_SIZE), 0), dt - 1)
        valid = lax.broadcasted_iota(jnp.int32, (nrow, PAGE_SIZE), 1) < qidx
        w, sps = _sb_weights(_dot_nt(qbd, kblk) + bias, jnp.zeros((nrow, 1), f32), valid, umat)
        acc_ref[...] = _dot(w, vblk)
        c_ref[...] = sps

    @pl.when(p > 0)
    def _():
        qbd = qbd_ref[...]
        c = c_ref[...]
        ws = []
        for j in range(group):
            kt = kc_refs[j][...].reshape(ODD_W, PAGE_SIZE).astype(bf16)
            w, sps = _sb_weights(_dot(qbd, kt) + bias, c, None, umat)
            ws.append(w)
            c = c + sps
        c_ref[...] = c
        vt = jnp.concatenate([vc_refs[j][...].reshape(ODD_W, PAGE_SIZE).astype(bf16) for j in range(group)], axis=1)
        acc_ref[...] += _dot_nt(jnp.concatenate(ws, axis=1), vt)

    @pl.when(p == nstep)
    def _():
        lane = lax.broadcasted_iota(jnp.int32, (1, ODD_W), 1) // DH_C
        out = jnp.zeros((dt, ODD_W), f32)
        for h in range(H_C):
            out = jnp.where(lane == h, acc_ref[h * dt:(h + 1) * dt, :], out)
        o_ref[...] = out


def _sb_sample(q_rows, kv_new, cache_k, cache_v, page_table, bias_col, o_prev, i, db, dt, row0):
    n = q_rows.shape[0]
    npage = page_table.shape[1]
    group = math.gcd(npage, SB_PAGES_PER_STEP)
    nstep = npage // group
    kc = jnp.transpose(cache_k, (0, 1, 3, 4, 2))
    vc = jnp.transpose(cache_v, (0, 1, 3, 4, 2))
    rb0 = row0 // dt
    nrow = H_C * dt

    def page(j):
        return lambda b, p, pt: (i, pt[b, npage - 1 - (jnp.maximum(p, 1) - 1) * group - j], 0, 0, 0)

    page_specs = [pl.BlockSpec((None, None, H_C, DH_C, PAGE_SIZE), page(j)) for j in range(group)]
    return pl.pallas_call(
        functools.partial(_sb_sample_body, dt=dt, nstep=nstep, group=group),
        grid_spec=pltpu.PrefetchScalarGridSpec(
            num_scalar_prefetch=1,
            grid=(db, nstep + 1),
            in_specs=[
                pl.BlockSpec((dt, ODD_W), lambda b, p, pt: (rb0 + b, 0)),
                pl.BlockSpec((dt, 2 * ODD_W), lambda b, p, pt: (b, 0)),
            ] + page_specs + page_specs + [
                pl.BlockSpec((nrow, 1), lambda b, p, pt: (0, 0)),
                pl.BlockSpec(memory_space=pl.ANY),
            ],
            out_specs=pl.BlockSpec((dt, ODD_W), lambda b, p, pt: (rb0 + b, 0)),
            scratch_shapes=[
                pltpu.VMEM((nrow, ODD_W), bf16),
                pltpu.VMEM((nrow, ODD_W), f32),
                pltpu.VMEM((nrow, 1), f32),
            ],
        ),
        out_shape=jax.ShapeDtypeStruct((n, ODD_W), f32),
        input_output_aliases={3 + 2 * group + 1: 0},
        compiler_params=_cp(("arbitrary", "arbitrary"), VMEM_LIMIT),
        name="sb_sample",
    )(page_table, q_rows, kv_new, *([kc] * group), *([vc] * group), bias_col, o_prev)


def _win_tail_body(k_ref, v_ref, *rest):
    o_ref = rest[-1]
    o_ref[0:B_W, :] = k_ref[...].T
    o_ref[B_W:2 * B_W, :] = v_ref[...].T


def _win_tail(p, stack, g, i, n_even, nb, t, clen):
    assert t >= clen
    tb = min(clen, 2 * LANE)
    assert clen % tb == 0 and (t - clen) % tb == 0
    rows = lambda c0: pl.BlockSpec((tb, B_W), lambda b, j: (b * (t // tb) + (t - clen) // tb + j, c0 // B_W))
    args = [p, p]
    in_specs = [rows(COL_KB + g * B_W), rows(COL_VB + g * B_W)]
    aliases = {}
    if stack is not None:
        args.append(stack)
        in_specs.append(pl.BlockSpec(memory_space=pl.ANY))
        aliases = {2: 0}
    return pl.pallas_call(
        _win_tail_body,
        grid=(nb, clen // tb),
        in_specs=in_specs,
        out_specs=pl.BlockSpec((None, None, 2 * B_W, tb), lambda b, j: (i, b, 0, j)),
        out_shape=jax.ShapeDtypeStruct((n_even, nb, 2 * B_W, clen), f32),
        input_output_aliases=aliases,
        compiler_params=_cp(("arbitrary", "arbitrary")),
        name=f"win_tail_g{g}",
    )(*args)


def _token_major(a):
    n_odd, nb, _, t = a.shape
    return jnp.transpose(a.reshape(n_odd, nb, H_C, DH_C, t), (0, 1, 4, 2, 3))


def _win_rows(c):
    n_even, db, _, clen = c.shape
    return jnp.transpose(c.reshape(n_even, db, 2, H_BG, DH_B, clen), (0, 1, 5, 2, 3, 4))


def kernel(x_prompt, x_sample, state_gla, cache_win_g0, cache_win_g1, cache_win_g2, cache_sb_k, cache_sb_v, page_table, norm_pre, norm_post, ffn_w_gate, ffn_w_up, ffn_w_down, even_w_in, gla_w_gate, gla_b_gate, gla_norm, even_w_out, odd_w_in, odd_w_out, sb_bias):
    nb, t, d = x_prompt.shape
    db, dt, _ = x_sample.shape
    depth = norm_pre.shape[0]
    n_p, n_s = nb * t, db * dt
    win_caches = tuple(jnp.transpose(c, (0, 1, 3, 4, 5, 2)).reshape(c.shape[0], db, 2 * B_W, c.shape[2])
                       for c in (cache_win_g0, cache_win_g1, cache_win_g2))
    assert dt == SUBLANE and n_p % (16 * SUBLANE) == 0 and n_s % 16 == 0

    wg = ffn_w_gate.astype(bf16)
    wu = ffn_w_up.astype(bf16)
    wd = ffn_w_down.astype(bf16)
    g0, g1 = COL_QB, COL_QB + GATE_RANK
    w_main = jnp.concatenate([even_w_in[:, :, :g0], even_w_in[:, :, g1:]], axis=2).astype(bf16)
    w_gate_in = jnp.pad(even_w_in[:, :, g0:g1], ((0, 0), (0, 0), (0, LANE - GATE_RANK))).astype(bf16)
    gla_gw = jnp.pad(gla_w_gate, ((0, 0), (0, LANE - GATE_RANK), (0, 0))).astype(bf16)
    gla_gb = gla_b_gate[:, None, :]
    gla_gn = gla_norm[:, None, :]
    w_out_e = even_w_out.astype(bf16)
    w_in_o = odd_w_in.astype(bf16)
    w_out_o = odd_w_out.astype(bf16)
    npre = norm_pre[:, :, None, :]
    npost = norm_post[:, :, None, :]
    bias_cols = jnp.repeat(sb_bias, dt, axis=1)[:, :, None]

    x = (x_prompt.reshape(n_p, d), x_sample.reshape(n_s, d))
    c_p = GLA_CHUNK if t % GLA_CHUNK == 0 else t
    c_s = GLA_CHUNK if dt % GLA_CHUNK == 0 else dt

    gla_p, gla_s = [], []
    win_p, win_s = [None for _ in DIL_PAIRS], [None for _ in DIL_PAIRS]
    kt_all, vt_all, sbk_s, sbv_s = None, None, [], []
    for l in range(depth):
        i = l // 2
        x = _ffn(x, npre, wg, wu, wd, npost, l, 0, n_p)
        if l % 2 == 0:
            p, la, *p_grp = _proj_even(x, npre, w_main, w_gate_in, gla_gw, gla_gb, l, i)
            oa_p, sp_new = _gla(p, la, jnp.zeros((nb, H_A, DK_A, DV_A), f32), gla_gn, i, nb, t // c_p, c_p, 0,
                                math.gcd(nb, GLA_SEQS_PER_STEP))
            oa_s, ss_new = _gla(p, la, state_gla[i], gla_gn, i, db, dt // c_s, c_s, n_p,
                                math.gcd(db, 2 * GLA_SEQS_PER_STEP))
            gla_p.append(sp_new)
            gla_s.append(ss_new)
            og, lg = [], []
            for g, (window, _) in enumerate(DIL_PAIRS):
                if g == 0:
                    o_g, l_g = _dil_prompt(p, (COL_QB, COL_KB, COL_VB), g, nb, t)
                else:
                    o_g, l_g = _dil_prompt(p_grp[g - 1], (0, B_W, 2 * B_W), g, nb, t)
                win_s[g], o_g, l_g = _dil_sample(p, win_caches[g], win_s[g], o_g, l_g, g, i, db, dt, n_p)
                og.append(o_g)
                lg.append(l_g)
                win_p[g] = _win_tail(p, win_p[g], g, i, win_caches[g].shape[0], nb, t, win_caches[g].shape[3])
            mix = dict(mix="even", mix_args=((oa_p, oa_s), og, lg), mix_w=w_out_e, i=i)
        else:
            q_rows, kt_all, vt_all, kv_new = _proj_odd(x, npre, w_in_o, l, i, nb, t, kt_all, vt_all)
            o = _sb_prompt(q_rows, kt_all, vt_all, sb_bias[i], i, nb, t)
            o = _sb_sample(q_rows, kv_new, cache_sb_k, cache_sb_v, page_table, bias_cols[i], o, i, db, dt, n_p)
            sbk_s.append(kv_new[:, :ODD_W].reshape(db, dt, H_C, DH_C))
            sbv_s.append(kv_new[:, ODD_W:].reshape(db, dt, H_C, DH_C))
            mix = dict(mix="odd", mix_args=(o,), mix_w=w_out_o, i=i)
        x = _ffn(x, npre, wg, wu, wd, npost, l, 1, n_p, split_out=(l == depth - 1), **mix)

    return (x[0].reshape(nb, t, d), x[1].reshape(db, dt, d),
            jnp.stack(gla_p), jnp.stack(gla_s),
            _win_rows(win_p[0]), _win_rows(win_p[1]), _win_rows(win_p[2]),
            _win_rows(win_s[0]), _win_rows(win_s[1]), _win_rows(win_s[2]),
            _token_major(kt_all), _token_major(vt_all), jnp.stack(sbk_s), jnp.stack(sbv_s))
```
